```python
import jax, jax.numpy as jnp
from jax import lax
import numpy as np

D_MODEL = 1024
BATCH = 8
SEQ = 4096
DEPTH = 1
DEC_BATCH = 32
DEC_SEQ = 4
PAST_LEN = 16384
PAGE_SIZE = 128

H_A = 8
DH_A = 64
W_A = H_A * DH_A
H_B = 8
DH_B = 64
W_B = H_B * DH_B
MOBA_BLOCK = 256
MOBA_TOPK = 3
Q_BLOCK = 128
DEEPNORM_ALPHA = (2.0 * DEPTH) ** 0.25
DEEPNORM_BETA = (8.0 * DEPTH) ** -0.25
LN_EPS = 1e-5
FORGET_BIAS_INIT = 2.0
ADA_SCALE = 0.5
IN_SIZES = (W_A, W_A, W_A, W_A, W_B, W_B, W_B, W_B, H_B, D_MODEL, D_MODEL)
N_IN = sum(IN_SIZES)

kernel_name = 'gated_moba_fox_deepnorm_decode_step'


def _split_points():
    pts, acc = [], 0
    for s in IN_SIZES[:-1]:
        acc += s
        pts.append(acc)
    return pts


def alibi_slopes(n):
    return 2.0 ** (-8.0 * (jnp.arange(n, dtype=jnp.float32) + 1.0) / n)


def layer_norm(r, g, b):
    rf = r.astype(jnp.float32)
    mu = jnp.mean(rf, axis=-1, keepdims=True)
    var = jnp.mean(jnp.square(rf - mu), axis=-1, keepdims=True)
    return ((rf - mu) * lax.rsqrt(var + LN_EPS) * g + b).astype(r.dtype)


def pad_to_block(t):
    p = (-t.shape[1]) % MOBA_BLOCK
    if p == 0:
        return t
    return jnp.pad(t, [(0, 0), (0, p)] + [(0, 0)] * (t.ndim - 2))


def gather_past(cache, l, page_table):
    pages = cache[l, page_table]
    return pages.reshape((page_table.shape[0], page_table.shape[1] * cache.shape[2]) + cache.shape[3:])


def project(x, c, w_ada, b_ada, w_in, b_f):
    B, T, _ = x.shape
    shift, scale, gate = jnp.split(c @ w_ada + b_ada, 3, axis=-1)
    h = x * (1.0 + scale[:, None, :]) + shift[:, None, :]
    qa, ka, va, za, qb, kb, vb, zb, fl, ga, gb = jnp.split(h @ w_in, _split_points(), axis=-1)
    ha = lambda t: t.reshape(B, T, H_A, DH_A)
    hb = lambda t: t.reshape(B, T, H_B, DH_B)
    logf = jax.nn.log_sigmoid((fl + b_f).astype(jnp.float32))
    return gate, ha(qa), ha(ka), ha(va), za, hb(qb), hb(kb), hb(vb), zb, logf, ga, gb


def moba_attention(q, k, v, q0):
    B, Tq, H, Dh = q.shape
    nb = k.shape[1] // MOBA_BLOCK
    kblk = k.reshape(B, nb, MOBA_BLOCK, H, Dh)
    vblk = v.reshape(B, nb, MOBA_BLOCK, H, Dh)
    kmean = jnp.mean(kblk.astype(jnp.float32), axis=2)
    qbs = min(Q_BLOCK, Tq)
    nqb = -(-Tq // qbs)
    qp = jnp.pad(q, ((0, 0), (0, nqb * qbs - Tq), (0, 0), (0, 0))).reshape(B * nqb, qbs, H, Dh)
    nsel = min(MOBA_TOPK, nb)
    slopes = alibi_slopes(H)
    scale = Dh ** -0.5
    s_ar = jnp.arange(MOBA_BLOCK)
    h_ix = jnp.arange(H)[:, None, None, None]
    blk_ar = jnp.arange(nb)

    def chunk(args):
        n, qc = args
        bi = n // nqb
        start = q0 + (n % nqb) * qbs
        own = start // MOBA_BLOCK
        tpos = start + jnp.arange(qbs)
        qf = qc.astype(jnp.float32)
        sc = jnp.einsum('qhd,nhd->hqn', qf, kmean[bi])
        sc = jnp.where(blk_ar < own, sc, -jnp.inf)
        top_s, idx = lax.top_k(sc, nsel)
        valid = jnp.isfinite(top_s)
        kg = kblk[bi, idx[..., None], s_ar, h_ix].astype(jnp.float32)
        vg = vblk[bi, idx[..., None], s_ar, h_ix].astype(jnp.float32)
        spos = idx[..., None] * MOBA_BLOCK + s_ar
        lp = (jnp.einsum('qhd,hqjsd->hqjs', qf, kg) * scale
              - slopes[:, None, None, None] * (tpos[None, :, None, None] - spos))
        lp = jnp.where(valid[..., None], lp, -jnp.inf).reshape(H, qbs, nsel * MOBA_BLOCK)
        ko = kblk[bi, own].astype(jnp.float32)
        vo = vblk[bi, own].astype(jnp.float32)
        opos = own * MOBA_BLOCK + s_ar
        lo = (jnp.einsum('qhd,shd->hqs', qf, ko) * scale
              - slopes[:, None, None] * (tpos[None, :, None] - opos[None, None, :]))
        lo = jnp.where(opos[None, None, :] <= tpos[None, :, None], lo, -jnp.inf)
        w = jax.nn.softmax(jnp.concatenate([lp, lo], axis=-1), axis=-1)
        wp = w[..., :nsel * MOBA_BLOCK].reshape(H, qbs, nsel, MOBA_BLOCK)
        wo = w[..., nsel * MOBA_BLOCK:]
        out = jnp.einsum('hqjs,hqjsd->qhd', wp, vg) + jnp.einsum('hqs,shd->qhd', wo, vo)
        return out.astype(q.dtype)

    out = lax.map(chunk, (jnp.arange(B * nqb), qp))
    return out.reshape(B, nqb * qbs, H, Dh)[:, :Tq]


def fox_attention(q, k, v, cq, ck, q0):
    B, Tq, H, Dh = q.shape
    Lk = k.shape[1]
    qbs = min(Q_BLOCK, Tq)
    nqb = -(-Tq // qbs)
    pad = nqb * qbs - Tq
    qp = jnp.pad(q, ((0, 0), (0, pad), (0, 0), (0, 0))).reshape(B, nqb, qbs, H, Dh).transpose(1, 0, 2, 3, 4)
    cqp = jnp.pad(cq, ((0, 0), (0, pad), (0, 0))).reshape(B, nqb, qbs, H).transpose(1, 0, 3, 2)
    kf = k.astype(jnp.float32)
    vf = v.astype(jnp.float32)
    ckT = jnp.swapaxes(ck, 1, 2).astype(jnp.float32)
    kpos = jnp.arange(Lk)
    scale = Dh ** -0.5

    def block(args):
        i, qc, cqc = args
        tpos = q0 + i * qbs + jnp.arange(qbs)
        lg = (jnp.einsum('bqhd,bkhd->bhqk', qc.astype(jnp.float32), kf) * scale
              + (cqc.astype(jnp.float32)[..., None] - ckT[:, :, None, :]))
        lg = jnp.where(kpos[None, :] <= tpos[:, None], lg, -jnp.inf)
        w = jax.nn.softmax(lg, axis=-1)
        return jnp.einsum('bhqk,bkhd->bqhd', w, vf).astype(q.dtype)

    out = lax.map(block, (jnp.arange(nqb), qp, cqp))
    return out.transpose(1, 0, 2, 3, 4).reshape(B, nqb * qbs, H, Dh)[:, :Tq]


def combine(x, gate, oa, za, ob, zb, ga, gb, w_pa, w_pb, w_o, ln_g, ln_b):
    B, T, _ = x.shape
    ya = (oa.reshape(B, T, W_A) * jax.nn.silu(za)) @ w_pa
    yb = (ob.reshape(B, T, W_B) * jax.nn.silu(zb)) @ w_pb
    s = (jax.nn.sigmoid(ga) * ya + jax.nn.sigmoid(gb) * yb) @ w_o
    return layer_norm(DEEPNORM_ALPHA * x + gate[:, None, :] * s, ln_g, ln_b)


def sample_moba(qa, ka, va, cache_k, cache_v, l, page_table):
    past_len = page_table.shape[1] * cache_k.shape[2]
    k_all = pad_to_block(jnp.concatenate([gather_past(cache_k, l, page_table), ka], axis=1))
    v_all = pad_to_block(jnp.concatenate([gather_past(cache_v, l, page_table), va], axis=1))
    return moba_attention(qa, k_all, v_all, past_len)


def sample_fox(qb, kb, vb, lf, cache_k, cache_v, cache_lf, l, page_table):
    past_len = page_table.shape[1] * cache_k.shape[2]
    k_all = jnp.concatenate([gather_past(cache_k, l, page_table), kb], axis=1)
    v_all = jnp.concatenate([gather_past(cache_v, l, page_table), vb], axis=1)
    lf_all = jnp.concatenate([gather_past(cache_lf, l, page_table).astype(jnp.float32), lf], axis=1)
    c_all = jnp.cumsum(lf_all, axis=1)
    return fox_attention(qb, k_all, v_all, c_all[:, past_len:], c_all, past_len)


def setup_inputs(seed: int = 0) -> dict:
    key = jax.random.key(seed)
    ks = jax.random.split(key, 20)
    n_pages = PAST_LEN // PAGE_SIZE
    n_used = DEC_BATCH * n_pages
    n_pool = n_used + (n_used + 3) // 4
    nrm = lambda k, shape, s=1.0: s * jax.random.normal(k, shape, jnp.float32)
    x_prompt = nrm(ks[0], (BATCH, SEQ, D_MODEL))
    x_sample = nrm(ks[1], (DEC_BATCH, DEC_SEQ, D_MODEL))
    cache_moba_k = nrm(ks[2], (DEPTH, n_pool, PAGE_SIZE, H_A, DH_A))
    cache_moba_v = nrm(ks[3], (DEPTH, n_pool, PAGE_SIZE, H_A, DH_A))
    cache_fox_k = nrm(ks[4], (DEPTH, n_pool, PAGE_SIZE, H_B, DH_B))
    cache_fox_v = nrm(ks[5], (DEPTH, n_pool, PAGE_SIZE, H_B, DH_B))
    cache_fox_logf = jax.nn.log_sigmoid(FORGET_BIAS_INIT + nrm(ks[6], (DEPTH, n_pool, PAGE_SIZE, H_B)))
    page_table = jax.random.permutation(ks[7], n_pool)[:n_used].reshape(DEC_BATCH, n_pages).astype(jnp.int32)
    c_prompt = nrm(ks[8], (BATCH, D_MODEL))
    c_sample = nrm(ks[9], (DEC_BATCH, D_MODEL))
    w_ada = nrm(ks[10], (DEPTH, D_MODEL, 3 * D_MODEL), ADA_SCALE * D_MODEL ** -0.5)
    b_ada = nrm(ks[11], (DEPTH, 3 * D_MODEL), 0.01)
    w_in = nrm(ks[12], (DEPTH, D_MODEL, N_IN), D_MODEL ** -0.5)
    b_f = FORGET_BIAS_INIT + nrm(ks[13], (DEPTH, H_B), 0.1)
    w_pa = nrm(ks[14], (DEPTH, W_A, D_MODEL), DEEPNORM_BETA * W_A ** -0.5)
    w_pb = nrm(ks[15], (DEPTH, W_B, D_MODEL), DEEPNORM_BETA * W_B ** -0.5)
    w_o = nrm(ks[16], (DEPTH, D_MODEL, D_MODEL), DEEPNORM_BETA * D_MODEL ** -0.5)
    ln_g = 1.0 + nrm(ks[17], (DEPTH, D_MODEL), 0.1)
    ln_b = nrm(ks[18], (DEPTH, D_MODEL), 0.1)
    return {'x_prompt': x_prompt, 'x_sample': x_sample,
            'cache_moba_k': cache_moba_k, 'cache_moba_v': cache_moba_v,
            'cache_fox_k': cache_fox_k, 'cache_fox_v': cache_fox_v, 'cache_fox_logf': cache_fox_logf,
            'page_table': page_table, 'c_prompt': c_prompt, 'c_sample': c_sample,
            'w_ada': w_ada, 'b_ada': b_ada, 'w_in': w_in, 'b_f': b_f,
            'w_pa': w_pa, 'w_pb': w_pb, 'w_o': w_o, 'ln_g': ln_g, 'ln_b': ln_b}


def reference(x_prompt, x_sample, cache_moba_k, cache_moba_v, cache_fox_k, cache_fox_v, cache_fox_logf,
              page_table, c_prompt, c_sample, w_ada, b_ada, w_in, b_f, w_pa, w_pb, w_o, ln_g, ln_b):
    x_p, x_s = x_prompt, x_sample
    mk_p, mv_p, fk_p, fv_p, fl_p = [], [], [], [], []
    mk_s, mv_s, fk_s, fv_s, fl_s = [], [], [], [], []
    for l in range(DEPTH):
        gate, qa, ka, va, za, qb, kb, vb, zb, lf, ga, gb = project(x_p, c_prompt, w_ada[l], b_ada[l], w_in[l], b_f[l])
        oa = moba_attention(qa, pad_to_block(ka), pad_to_block(va), 0)
        cum = jnp.cumsum(lf, axis=1)
        ob = fox_attention(qb, kb, vb, cum, cum, 0)
        x_p = combine(x_p, gate, oa, za, ob, zb, ga, gb, w_pa[l], w_pb[l], w_o[l], ln_g[l], ln_b[l])
        mk_p.append(ka); mv_p.append(va); fk_p.append(kb); fv_p.append(vb); fl_p.append(lf)
        gs, qa_s, ka_s, va_s, za_s, qb_s, kb_s, vb_s, zb_s, lf_s, ga_s, gb_s = project(
            x_s, c_sample, w_ada[l], b_ada[l], w_in[l], b_f[l])
        oa_s = sample_moba(qa_s, ka_s, va_s, cache_moba_k, cache_moba_v, l, page_table)
        ob_s = sample_fox(qb_s, kb_s, vb_s, lf_s, cache_fox_k, cache_fox_v, cache_fox_logf, l, page_table)
        x_s = combine(x_s, gs, oa_s, za_s, ob_s, zb_s, ga_s, gb_s, w_pa[l], w_pb[l], w_o[l], ln_g[l], ln_b[l])
        mk_s.append(ka_s); mv_s.append(va_s); fk_s.append(kb_s); fv_s.append(vb_s); fl_s.append(lf_s)
    return (x_p, x_s,
            jnp.stack(mk_p), jnp.stack(mv_p), jnp.stack(fk_p), jnp.stack(fv_p), jnp.stack(fl_p),
            jnp.stack(mk_s), jnp.stack(mv_s), jnp.stack(fk_s), jnp.stack(fv_s), jnp.stack(fl_s))
```

```python
import functools

import jax
import jax.numpy as jnp
from jax import lax
from jax.experimental import pallas as pl
from jax.experimental.pallas import tpu as pltpu

F32 = jnp.float32
BF16 = jnp.bfloat16

N_HEADS = 8
D_HEAD = 64
BRANCH_W = N_HEADS * D_HEAD
MOBA_BLOCK = 256
MOBA_TOPK = 3
LN_EPS = 1e-5
LANES = 128
V_ROWS = 80
VMEM_LIMIT = 56 * 1024 * 1024
NEG_INF = float("-inf")


def _cparams(*sem):
    return pltpu.CompilerParams(dimension_semantics=sem, vmem_limit_bytes=VMEM_LIMIT)


def _split3(x):
    hi = x.astype(BF16).astype(F32)
    r = x - hi
    mid = r.astype(BF16).astype(F32)
    lo = (r - mid).astype(BF16).astype(F32)
    return hi, mid, lo


def _lane_cumsum(x, seg=None):
    n = x.shape[-1]
    lane = lax.broadcasted_iota(jnp.int32, x.shape, x.ndim - 1)
    pos = lane if seg is None else lane & (seg - 1)
    limit = n if seg is None else seg
    s = 1
    while s < limit:
        x = x + jnp.where(pos >= s, pltpu.roll(x, s, axis=x.ndim - 1), 0.0)
        s *= 2
    return x


def _log_sigmoid(x):
    return jnp.minimum(x, 0.0) - jnp.log(1.0 + jnp.exp(-jnp.abs(x)))


def _sigmoid(x):
    return 1.0 / (1.0 + jnp.exp(-x))


def _dot(a, b):
    return jnp.dot(a, b, preferred_element_type=F32)


def _dot_nt(a, b):
    return lax.dot_general(a, b, (((1,), (1,)), ((), ())), preferred_element_type=F32)


def _ada_body(c_ref, w_ref, b_ref, o_ref):
    o_ref[...] = lax.dot_general(c_ref[...], w_ref[...], (((1,), (0,)), ((), ())),
                                 precision=lax.Precision.HIGHEST,
                                 preferred_element_type=F32) + b_ref[...]


def _ada(c_all, w_ada, b_ada):
    n, d = c_all.shape
    d3 = w_ada.shape[1]
    tn = 1024
    return pl.pallas_call(
        _ada_body,
        grid=(d3 // tn,),
        in_specs=[pl.BlockSpec((n, d), lambda j: (0, 0)),
                  pl.BlockSpec((d, tn), lambda j: (0, j)),
                  pl.BlockSpec((1, tn), lambda j: (0, j))],
        out_specs=pl.BlockSpec((n, tn), lambda j: (0, j)),
        out_shape=jax.ShapeDtypeStruct((n, d3), F32),
        compiler_params=_cparams("arbitrary"),
        name="ada",
    )(c_all, w_ada, b_ada.reshape(1, d3))


def _split2(x):
    hi = x.astype(BF16)
    return hi, (x - hi.astype(F32)).astype(BF16)


def _forget_logits_t(wft_ref, hb, hl):
    w_hi, w_lo = _split2(wft_ref[...])
    return (_dot_nt(w_hi, hb) + _dot_nt(w_hi, hl) + _dot_nt(w_lo, hb))[:N_HEADS]


def _proj_body(x_ref, sc_ref, sh_ref, wkk_ref, wt_ref, wft_ref, bf_ref,
               kat_ref, vat_ref, kbt_ref, vbt_ref, qat_ref, qbt_ref, vag_ref, vbg_ref,
               kar_ref, kbr_ref, lft_ref, ct_ref, fkaug_ref, kmean_ref, carry_ref, *, tm, tk):
    t = pl.program_id(1)

    @pl.when(t == 0)
    def _():
        carry_ref[...] = jnp.zeros_like(carry_ref)

    h32 = x_ref[0] * (1.0 + sc_ref[0]) + sh_ref[0]
    hb = h32.astype(BF16)
    hl = (h32 - hb.astype(F32)).astype(BF16)

    r1 = _dot(hb, wkk_ref[...])
    kar_ref[0] = r1[:, :BRANCH_W].astype(BF16)
    kbr_ref[0] = r1[:, BRANCH_W:].astype(BF16)
    nblk = tm // MOBA_BLOCK
    km = jnp.mean(r1[:, :BRANCH_W].reshape(nblk, MOBA_BLOCK, BRANCH_W), axis=1)
    kmean_ref[0, 0] = jnp.zeros(kmean_ref.shape[2:], F32)
    kmean_ref[0, 0, 0:nblk, :] = km

    ones_rows = jnp.where(lax.broadcasted_iota(jnp.int32, (V_ROWS - D_HEAD, tk), 0) == 0, 1.0, 0.0).astype(BF16)

    def group(g):
        return _dot_nt(wt_ref[g * BRANCH_W:(g + 1) * BRANCH_W, :], hb)

    def store_v(r, vt_ref, vg_ref):
        vt_ref[0] = r
        rb = r.astype(BF16)
        for j in range(tm // tk):
            for h in range(N_HEADS):
                vg_ref[0, j, h * V_ROWS:h * V_ROWS + D_HEAD, :] = rb[h * D_HEAD:(h + 1) * D_HEAD, j * tk:(j + 1) * tk]
                vg_ref[0, j, h * V_ROWS + D_HEAD:(h + 1) * V_ROWS, :] = ones_rows

    scale = D_HEAD ** -0.5
    qat_ref[0] = (group(0) * scale).astype(BF16)
    kat_ref[0] = group(1)
    store_v(group(2), vat_ref, vag_ref)
    qbt_ref[0] = (group(3) * scale).astype(BF16)
    kbt_ref[0] = group(4)
    store_v(group(5), vbt_ref, vbg_ref)

    lf = _log_sigmoid(_forget_logits_t(wft_ref, hb, hl) + bf_ref[:, 0:1])
    lft_ref[0] = lf
    c = _lane_cumsum(lf) + carry_ref[:, 0:1]
    ct_ref[0] = c
    carry_ref[...] = jnp.broadcast_to(c[:, tm - 1:tm], carry_ref.shape)

    hi, mid, lo = _split3(-c)
    ones_blk = jnp.where(lax.broadcasted_iota(jnp.int32, (N_HEADS, tm), 0) < 3, 1.0, 0.0)
    aug_t = jnp.concatenate([hi, mid, lo, ones_blk, jnp.zeros((LANES - 4 * N_HEADS, tm), F32)], axis=0)
    fkaug_ref[0] = aug_t.T.astype(BF16)


def _project_prompt(x, scale, shift, wp, tm, tk):
    b, t, d = x.shape
    nt = t // tm
    f32o = lambda rows: jax.ShapeDtypeStruct((b, rows, t), F32)
    out_shape = (
        f32o(BRANCH_W), f32o(BRANCH_W), f32o(BRANCH_W), f32o(BRANCH_W),
        jax.ShapeDtypeStruct((b, BRANCH_W, t), BF16), jax.ShapeDtypeStruct((b, BRANCH_W, t), BF16),
        jax.ShapeDtypeStruct((b, t // tk, N_HEADS * V_ROWS, tk), BF16),
        jax.ShapeDtypeStruct((b, t // tk, N_HEADS * V_ROWS, tk), BF16),
        jax.ShapeDtypeStruct((b, t, BRANCH_W), BF16), jax.ShapeDtypeStruct((b, t, BRANCH_W), BF16),
        f32o(N_HEADS), f32o(N_HEADS),
        jax.ShapeDtypeStruct((b, t, LANES), BF16),
        jax.ShapeDtypeStruct((b, nt, 8, BRANCH_W), F32),
    )
    tmap = lambda i, j: (i, 0, j)
    rmap = lambda i, j: (i, j, 0)
    vg_spec = pl.BlockSpec((1, tm // tk, N_HEADS * V_ROWS, tk), lambda i, j: (i, j, 0, 0))
    const = lambda i, j: (0, 0)
    out_specs = (
        pl.BlockSpec((1, BRANCH_W, tm), tmap), pl.BlockSpec((1, BRANCH_W, tm), tmap),
        pl.BlockSpec((1, BRANCH_W, tm), tmap), pl.BlockSpec((1, BRANCH_W, tm), tmap),
        pl.BlockSpec((1, BRANCH_W, tm), tmap), pl.BlockSpec((1, BRANCH_W, tm), tmap),
        vg_spec, vg_spec,
        pl.BlockSpec((1, tm, BRANCH_W), rmap), pl.BlockSpec((1, tm, BRANCH_W), rmap),
        pl.BlockSpec((1, N_HEADS, tm), tmap), pl.BlockSpec((1, N_HEADS, tm), tmap),
        pl.BlockSpec((1, tm, LANES), rmap),
        pl.BlockSpec((1, 1, 8, BRANCH_W), lambda i, j: (i, j, 0, 0)),
    )
    in_specs = [
        pl.BlockSpec((1, tm, d), rmap),
        pl.BlockSpec((1, 1, d), lambda i, j: (i, 0, 0)),
        pl.BlockSpec((1, 1, d), lambda i, j: (i, 0, 0)),
        pl.BlockSpec(wp["w_kk"].shape, const),
        pl.BlockSpec(wp["w_t"].shape, const),
        pl.BlockSpec(wp["wft"].shape, const),
        pl.BlockSpec(wp["bf_col"].shape, const),
    ]
    return pl.pallas_call(
        functools.partial(_proj_body, tm=tm, tk=tk),
        grid=(b, nt),
        in_specs=in_specs,
        out_specs=out_specs,
        out_shape=out_shape,
        scratch_shapes=[pltpu.VMEM((N_HEADS, LANES), F32)],
        compiler_params=_cparams("arbitrary", "arbitrary"),
        name="proj_prompt",
    )(x, scale, shift, wp["w_kk"], wp["w_t"], wp["wft"], wp["bf_col"])


def _masked_q(q_ref, hh, tq):
    qp = q_ref[0]
    z = jnp.zeros((D_HEAD, tq), BF16)
    return jnp.concatenate([qp[:D_HEAD], z], axis=0) if hh == 0 else jnp.concatenate([z, qp[D_HEAD:]], axis=0)


def _flash_head(scores, v_tile, qi, diag_mask, past_bias):
    s = jnp.where(diag_mask, scores(qi), NEG_INF)
    m = jnp.max(s, axis=0, keepdims=True)
    p = jnp.exp(s - m).astype(BF16)
    acc = _dot(v_tile(qi), p)

    def body(kv, carry):
        m, acc = carry
        s = scores(kv)
        if past_bias is not None:
            s = s + past_bias(kv)
        m_new = jnp.maximum(m, jnp.max(s, axis=0, keepdims=True))
        alpha = jnp.exp(m - m_new)
        p = jnp.exp(s - m_new).astype(BF16)
        return m_new, alpha * acc + _dot(v_tile(kv), p)

    m, acc = lax.fori_loop(0, qi, body, (m, acc))
    return acc[:D_HEAD] / acc[D_HEAD:D_HEAD + 1]


def _fox_body(q_ref, k_ref, ka_ref, v_ref, c_ref, o_ref, rhs_ref, ot_ref, *, tq, tk):
    hp = pl.program_id(1)
    qi = pl.program_id(2)
    rowi = lax.broadcasted_iota(jnp.int32, (LANES, tq), 0)
    causal = (lax.broadcasted_iota(jnp.int32, (tk, tq), 0) <= lax.broadcasted_iota(jnp.int32, (tk, tq), 1))
    for hh in range(2):
        h = hp * 2 + hh
        hi, mid, lo = _split3(c_ref[0, pl.ds(h, 1), :])
        aug = jnp.where((rowi < 3 * N_HEADS) & ((rowi & (N_HEADS - 1)) == h), 1.0, 0.0)
        aug = jnp.where(rowi == 3 * N_HEADS, hi, aug)
        aug = jnp.where(rowi == 3 * N_HEADS + 1, mid, aug)
        aug = jnp.where(rowi == 3 * N_HEADS + 2, lo, aug)
        rhs_ref[0:LANES, :] = _masked_q(q_ref, hh, tq)
        rhs_ref[LANES:2 * LANES, :] = aug.astype(BF16)

        def scores(kv):
            off = pl.multiple_of(kv * tk, tk)
            lhs = jnp.concatenate([k_ref[0, pl.ds(off, tk), :], ka_ref[0, pl.ds(off, tk), :]], axis=1)
            return _dot(lhs, rhs_ref[...])

        def v_tile(kv):
            return v_ref[0, kv, hh * V_ROWS:(hh + 1) * V_ROWS, :]

        ot_ref[hh * D_HEAD:(hh + 1) * D_HEAD, :] = _flash_head(scores, v_tile, qi, causal, None)
    o_ref[0] = ot_ref[...].T.astype(o_ref.dtype)


def _fox_prompt(qt, kr, kaug, vg, ct, tq, tk):
    b, _, t = qt.shape
    nk = t // tk
    return pl.pallas_call(
        functools.partial(_fox_body, tq=tq, tk=tk),
        grid=(b, N_HEADS // 2, t // tq),
        in_specs=[
            pl.BlockSpec((1, LANES, tq), lambda i, p, j: (i, p, j)),
            pl.BlockSpec((1, t, LANES), lambda i, p, j: (i, 0, p)),
            pl.BlockSpec((1, t, LANES), lambda i, p, j: (i, 0, 0)),
            pl.BlockSpec((1, nk, 2 * V_ROWS, tk), lambda i, p, j: (i, 0, p, 0)),
            pl.BlockSpec((1, N_HEADS, tq), lambda i, p, j: (i, 0, j)),
        ],
        out_specs=pl.BlockSpec((1, tq, LANES), lambda i, p, j: (i, j, p)),
        out_shape=jax.ShapeDtypeStruct((b, t, BRANCH_W), BF16),
        scratch_shapes=[pltpu.VMEM((2 * LANES, tq), BF16), pltpu.VMEM((LANES, tq), F32)],
        compiler_params=_cparams("arbitrary", "arbitrary", "arbitrary"),
        name="fox_prompt",
    )(qt, kr, kaug, vg, ct)


def _moba_body(sl_ref, q_ref, k_ref, v_ref, km_ref, o_ref, rhs_ref, sel_ref, ot_ref, pa_ref, *, tq, tk, nb):
    hp = pl.program_id(1)
    qi = pl.program_id(2)

    @pl.when((pl.program_id(0) == 0) & (hp == 0) & (qi == 0))
    def _():
        shape = pa_ref.shape
        pos = lax.broadcasted_iota(jnp.int32, shape, 0).astype(F32)
        col = lax.broadcasted_iota(jnp.int32, shape, 1)
        p_hi = pos.astype(BF16).astype(F32)
        pa_ref[...] = jnp.where(col < 3, p_hi, jnp.where(col < 6, pos - p_hi, jnp.where(col < 9, 1.0, 0.0))).astype(BF16)
    rowi = lax.broadcasted_iota(jnp.int32, (LANES, tq), 0)
    causal = (lax.broadcasted_iota(jnp.int32, (tk, tq), 0) <= lax.broadcasted_iota(jnp.int32, (tk, tq), 1))
    blk = lax.broadcasted_iota(jnp.int32, (nb, tq), 0)
    tpos = (qi * tq + lax.broadcasted_iota(jnp.int32, (1, tq), 1)).astype(F32)
    for hh in range(2):
        h = hp * 2 + hh
        qm = _masked_q(q_ref, hh, tq)

        km = km_ref[0]
        km_hi = km.astype(BF16)
        km_lo = (km - km_hi.astype(F32)).astype(BF16)
        sc = _dot(km_hi, qm) + _dot(km_lo, qm)
        rank = jnp.zeros((nb, tq), F32)
        for mm in range(nb):
            row = sc[mm:mm + 1, :]
            beats = (row > sc) | ((row == sc) & (blk > mm))
            rank = rank + jnp.where(beats, 1.0, 0.0) * (mm < qi).astype(F32)
        chosen = (blk < qi) & (rank < MOBA_TOPK)
        sel_ref[...] = jnp.where(chosen, 0.0, NEG_INF)

        slope = jnp.full((1, tq), sl_ref[h], F32)
        s_hi, s_mid, s_lo = _split3(slope)
        a_hi, a_mid, a_lo = _split3(-(slope * tpos))
        aug = jnp.where((rowi == 0) | (rowi == 3), s_hi, 0.0)
        aug = jnp.where((rowi == 1) | (rowi == 4), s_mid, aug)
        aug = jnp.where((rowi == 2) | (rowi == 5), s_lo, aug)
        aug = jnp.where(rowi == 6, a_hi, aug)
        aug = jnp.where(rowi == 7, a_mid, aug)
        aug = jnp.where(rowi == 8, a_lo, aug)
        rhs_ref[0:LANES, :] = qm
        rhs_ref[LANES:2 * LANES, :] = aug.astype(BF16)

        def scores(kv):
            off = pl.multiple_of(kv * tk, tk)
            lhs = jnp.concatenate([k_ref[0, pl.ds(off, tk), :], pa_ref[pl.ds(off, tk), :]], axis=1)
            return _dot(lhs, rhs_ref[...])

        def v_tile(kv):
            return v_ref[0, kv, hh * V_ROWS:(hh + 1) * V_ROWS, :]

        def past_bias(kv):
            return sel_ref[pl.ds(kv, 1), :]

        ot_ref[hh * D_HEAD:(hh + 1) * D_HEAD, :] = _flash_head(scores, v_tile, qi, causal, past_bias)
    o_ref[0] = ot_ref[...].T.astype(o_ref.dtype)


def _moba_prompt(slopes, qt, kr, vg, kmean, tq, tk):
    b, _, t = qt.shape
    nk = t // tk
    nb = kmean.shape[1]
    return pl.pallas_call(
        functools.partial(_moba_body, tq=tq, tk=tk, nb=nb),
        grid=(b, N_HEADS // 2, t // tq),
        in_specs=[
            pl.BlockSpec(memory_space=pltpu.SMEM),
            pl.BlockSpec((1, LANES, tq), lambda i, p, j: (i, p, j)),
            pl.BlockSpec((1, t, LANES), lambda i, p, j: (i, 0, p)),
            pl.BlockSpec((1, nk, 2 * V_ROWS, tk), lambda i, p, j: (i, 0, p, 0)),
            pl.BlockSpec((1, nb, LANES), lambda i, p, j: (i, 0, p)),
        ],
        out_specs=pl.BlockSpec((1, tq, LANES), lambda i, p, j: (i, j, p)),
        scratch_shapes=[pltpu.VMEM((2 * LANES, tq), BF16), pltpu.VMEM((nb, tq), F32),
                        pltpu.VMEM((LANES, tq), F32), pltpu.VMEM((t, LANES), BF16)],
        out_shape=jax.ShapeDtypeStruct((b, t, BRANCH_W), BF16),
        compiler_params=_cparams("arbitrary", "arbitrary", "arbitrary"),
        name="moba_prompt",
    )(slopes, qt, kr, vg, kmean)


def _combine_body(x_ref, sc_ref, sh_ref, g_ref, oa_ref, ob_ref, wz_ref, wg_ref, wpa_ref, wpb_ref, wo_ref,
                  lng_ref, lnb_ref, y_ref, *, alpha):
    x = x_ref[0]
    hb = (x * (1.0 + sc_ref[0]) + sh_ref[0]).astype(BF16)
    z = _dot(hb, wz_ref[...])
    g = _dot(hb, wg_ref[...])
    za, zb = z[:, :BRANCH_W], z[:, BRANCH_W:]
    d = x.shape[-1]
    ua = (oa_ref[0].astype(F32) * (za * _sigmoid(za))).astype(BF16)
    ub = (ob_ref[0].astype(F32) * (zb * _sigmoid(zb))).astype(BF16)
    ya = _dot(ua, wpa_ref[...])
    yb = _dot(ub, wpb_ref[...])
    mix = (_sigmoid(g[:, :d]) * ya + _sigmoid(g[:, d:]) * yb).astype(BF16)
    s = _dot(mix, wo_ref[...])
    r = alpha * x + g_ref[0] * s
    mu = jnp.mean(r, axis=-1, keepdims=True)
    dev = r - mu
    var = jnp.mean(dev * dev, axis=-1, keepdims=True)
    y_ref[0] = dev * lax.rsqrt(var + LN_EPS) * lng_ref[...] + lnb_ref[...]


def _combine(x, scale, shift, gate, oa, ob, wp, alpha, tm):
    b, t, d = x.shape
    tmod = scale.shape[1]
    rmap = lambda i, j: (i, j, 0)
    mod_spec = pl.BlockSpec((1, 1, d), lambda i, j: (i, 0, 0)) if tmod == 1 else pl.BlockSpec((1, tm, d), rmap)
    const = lambda i, j: (0, 0)
    names = ("w_z", "w_g", "w_pa", "w_pb", "w_o", "ln_g", "ln_b")
    return pl.pallas_call(
        functools.partial(_combine_body, alpha=alpha),
        grid=(b, t // tm),
        in_specs=[pl.BlockSpec((1, tm, d), rmap), mod_spec, mod_spec, mod_spec,
                  pl.BlockSpec((1, tm, BRANCH_W), rmap), pl.BlockSpec((1, tm, BRANCH_W), rmap)]
                 + [pl.BlockSpec(wp[k].shape, const) for k in names],
        out_specs=pl.BlockSpec((1, tm, d), rmap),
        out_shape=jax.ShapeDtypeStruct((b, t, d), F32),
        compiler_params=_cparams("arbitrary", "arbitrary"),
        name="combine",
    )(x, scale, shift, gate, oa, ob, *[wp[k] for k in names])


def _proj_sample_body(x_ref, sc_ref, sh_ref, wr_ref, wt_ref, wft_ref, wfr_ref, bf_ref, bfr_ref,
                      qa_ref, ka_ref, va_ref, qb_ref, kb_ref, vb_ref, lfr_ref,
                      kat_ref, vat_ref, kbt_ref, vbt_ref, lft_ref):
    h32 = x_ref[...] * (1.0 + sc_ref[...]) + sh_ref[...]
    hb = h32.astype(BF16)
    hl = (h32 - hb.astype(F32)).astype(BF16)
    r = _dot(hb, wr_ref[...])
    scale = D_HEAD ** -0.5
    w = BRANCH_W
    qa_ref[...] = r[:, 0:w] * scale
    ka_ref[...] = r[:, w:2 * w]
    va_ref[...] = r[:, 2 * w:3 * w]
    qb_ref[...] = r[:, 3 * w:4 * w] * scale
    kb_ref[...] = r[:, 4 * w:5 * w]
    vb_ref[...] = r[:, 5 * w:6 * w]
    wr_hi, wr_lo = _split2(wfr_ref[...])
    flr = _dot(hb, wr_hi) + _dot(hl, wr_hi) + _dot(hb, wr_lo)
    lfr_ref[...] = _log_sigmoid(flr + bfr_ref[...])
    for g, ref in ((1, kat_ref), (2, vat_ref), (4, kbt_ref), (5, vbt_ref)):
        ref[...] = _dot_nt(wt_ref[g * w:(g + 1) * w, :], hb)
    lft_ref[...] = _log_sigmoid(_forget_logits_t(wft_ref, hb, hl) + bf_ref[:, 0:1])


def _project_sample(x, scale, shift, wp):
    n, d = x.shape
    row = jax.ShapeDtypeStruct((n, BRANCH_W), F32)
    col = jax.ShapeDtypeStruct((BRANCH_W, n), F32)
    out_shape = (row,) * 6 + (jax.ShapeDtypeStruct((n, LANES), F32),) + (col,) * 4 + (
        jax.ShapeDtypeStruct((N_HEADS, n), F32),)
    args = (x, scale, shift, wp["w_rows"], wp["w_t"], wp["wft"], wp["wfr"], wp["bf_col"], wp["bf_row"])
    return pl.pallas_call(
        _proj_sample_body,
        out_shape=out_shape,
        compiler_params=pltpu.CompilerParams(vmem_limit_bytes=VMEM_LIMIT),
        name="proj_sample",
    )(*args)


def _shift_right(n):
    return n.bit_length() - 1


def _sample_body(pt_ref, qa_ref, qb_ref, *refs, past_len, nblk, ppb, dec_seq, page):
    caches = refs[:5 * ppb]
    mk, mv, fk, fv, fl = (caches[i * ppb:(i + 1) * ppb] for i in range(5))
    kan_ref, van_ref, kbn_ref, vbn_ref, lfn_ref, slope_ref, oa_ref, ob_ref = refs[5 * ppb:5 * ppb + 8]
    qda_ref, qdb_ref, ssc_ref, sm_ref, sl_ref, so_ref, fm_ref, fls_ref, facc_ref, fcar_ref = refs[5 * ppb + 8:]
    del pt_ref
    b = pl.program_id(0)
    n = pl.program_id(1)
    nrow = dec_seq * N_HEADS
    nkey = ppb * page
    ntok = kan_ref.shape[1]
    qsh = _shift_right(N_HEADS)

    def rows_iota(shape):
        return lax.broadcasted_iota(jnp.int32, shape, 0)

    def lanes_iota(shape):
        return lax.broadcasted_iota(jnp.int32, shape, 1)

    head_mask = (lanes_iota((nrow, BRANCH_W)) >> _shift_right(D_HEAD)) == (rows_iota((nrow, BRANCH_W)) & (N_HEADS - 1))

    def block_diag_q(q_ref):
        q = q_ref[0]
        q32 = jnp.concatenate([jnp.broadcast_to(q[i:i + 1, :], (N_HEADS, BRANCH_W)) for i in range(dec_seq)], axis=0)
        q32 = jnp.where(head_mask, q32, 0.0)
        hi = q32.astype(BF16)
        lo = (q32 - hi.astype(F32)).astype(BF16)
        return jnp.concatenate([hi, lo], axis=0)

    @pl.when(n == 0)
    def _():
        qda_ref[...] = block_diag_q(qa_ref)
        qdb_ref[...] = block_diag_q(qb_ref)
        ssc_ref[...] = jnp.full(ssc_ref.shape, NEG_INF, F32)
        sm_ref[...] = jnp.full(sm_ref.shape, NEG_INF, F32)
        sl_ref[...] = jnp.zeros(sl_ref.shape, F32)
        so_ref[...] = jnp.zeros(so_ref.shape, F32)
        fm_ref[...] = jnp.full(fm_ref.shape, NEG_INF, F32)
        fls_ref[...] = jnp.zeros(fls_ref.shape, F32)
        facc_ref[...] = jnp.zeros(facc_ref.shape, F32)
        fcar_ref[...] = jnp.zeros(fcar_ref.shape, F32)

    def qk(qd_ref, kt):
        s2 = _dot(qd_ref[...], kt)
        return s2[:nrow] + s2[nrow:]

    def bcast(col):
        return jnp.broadcast_to(col, (nrow, LANES))

    def softmax_part(lg):
        m = jnp.max(lg, axis=1, keepdims=True)
        p = jnp.exp(lg - m).astype(BF16)
        return m, p, jnp.sum(p.astype(F32), axis=1, keepdims=True)

    slope = slope_ref[:, 0:1]
    qrow = rows_iota((nrow, nkey)) >> qsh

    kt = jnp.concatenate([r[0] for r in mk], axis=1).astype(BF16)
    s = qk(qda_ref, kt)
    score = jnp.sum(s, axis=1, keepdims=True)
    dist = (past_len - n * nkey + qrow - lanes_iota((nrow, nkey))).astype(F32)
    m_n, p, l_n = softmax_part(s - slope * dist)
    vt = jnp.concatenate([r[0] for r in mv], axis=1).astype(BF16)
    o_n = _dot_nt(p, vt)
    new = (bcast(score), bcast(m_n), bcast(l_n), o_n)
    state = (ssc_ref, sm_ref, sl_ref, so_ref)
    gt = [score > ssc_ref[i][:, 0:1] for i in range(MOBA_TOPK)]
    old = [[ref[i] for ref in state] for i in range(MOBA_TOPK)]
    for i in range(MOBA_TOPK - 1, -1, -1):
        for j, ref in enumerate(state):
            keep = jnp.where(gt[i], new[j], old[i][j])
            ref[i] = keep if i == 0 else jnp.where(gt[i - 1], old[i - 1][j], keep)

    lf = jnp.concatenate([r[0] for r in fl], axis=1)
    c_blk = _lane_cumsum(lf) + fcar_ref[:, 0:1]
    fcar_ref[...] = jnp.broadcast_to(c_blk[:, nkey - 1:nkey], fcar_ref.shape)
    c_rows = jnp.concatenate([c_blk] * dec_seq, axis=0)
    kt = jnp.concatenate([r[0] for r in fk], axis=1).astype(BF16)
    u = qk(qdb_ref, kt) - c_rows
    m_old = fm_ref[:, 0:1]
    m_new = jnp.maximum(m_old, jnp.max(u, axis=1, keepdims=True))
    alpha = jnp.exp(m_old - m_new)
    p = jnp.exp(u - m_new).astype(BF16)
    vt = jnp.concatenate([r[0] for r in fv], axis=1).astype(BF16)
    fls_ref[...] = bcast(alpha * fls_ref[:, 0:1] + jnp.sum(p.astype(F32), axis=1, keepdims=True))
    facc_ref[...] = alpha * facc_ref[...] + _dot_nt(p, vt)
    fm_ref[...] = bcast(m_new)

    @pl.when(n == nblk - 1)
    def _():
        lane = lanes_iota((nrow, ntok))
        qr = rows_iota((nrow, ntok)) >> qsh
        tok_q = lane & (dec_seq - 1)
        visible = ((lane >> _shift_right(dec_seq)) == b) & (tok_q <= qr)

        def collapse(o):
            return jnp.sum(jnp.where(head_mask, o, 0.0).reshape(dec_seq, N_HEADS, BRANCH_W), axis=1)

        s_own = qk(qda_ref, kan_ref[...].astype(BF16))
        lg = jnp.where(visible, s_own - slope * (qr - tok_q).astype(F32), NEG_INF)
        m_o, p_o, l_o = softmax_part(lg)
        o_o = _dot_nt(p_o, van_ref[...].astype(BF16))
        ms = [sm_ref[i][:, 0:1] for i in range(MOBA_TOPK)]
        m_tot = m_o
        for mi in ms:
            m_tot = jnp.maximum(m_tot, mi)
        w_o = jnp.exp(m_o - m_tot)
        num = w_o * o_o
        den = w_o * l_o
        for i in range(MOBA_TOPK):
            w_i = jnp.exp(ms[i] - m_tot)
            num = num + w_i * so_ref[i]
            den = den + w_i * sl_ref[i][:, 0:1]
        oa_ref[0] = collapse(num / den)

        c_new = _lane_cumsum(lfn_ref[...], seg=dec_seq)
        c_new_rows = jnp.concatenate([c_new] * dec_seq, axis=0)
        cq = jnp.sum(jnp.where(lane == b * dec_seq + qr, c_new_rows, 0.0), axis=1, keepdims=True)
        c_past = jnp.concatenate([fcar_ref[:, 0:1]] * dec_seq, axis=0)
        m_p = fm_ref[:, 0:1] + c_past + cq
        s_new = qk(qdb_ref, kbn_ref[...].astype(BF16))
        lg = jnp.where(visible, s_new + cq - c_new_rows, NEG_INF)
        m_o, p_o, l_o = softmax_part(lg)
        o_o = _dot_nt(p_o, vbn_ref[...].astype(BF16))
        m_tot = jnp.maximum(m_p, m_o)
        w_p = jnp.exp(m_p - m_tot)
        w_o = jnp.exp(m_o - m_tot)
        num = w_p * facc_ref[...] + w_o * o_o
        den = w_p * fls_ref[:, 0:1] + w_o * l_o
        ob_ref[0] = collapse(num / den)


def _sample_attention(page_table, qa, qb, caches, new_t, slope_rows, past_len, page):
    nb_, dec_seq, _ = qa.shape
    ppb = MOBA_BLOCK // page
    nblk = past_len // MOBA_BLOCK
    nrow = dec_seq * N_HEADS

    def page_specs(rows):
        return [pl.BlockSpec((1, rows, page), functools.partial(lambda i, n, pt, j: (pt[i, n * ppb + j], 0, 0), j=j))
                for j in range(ppb)]

    q_spec = pl.BlockSpec((1, dec_seq, BRANCH_W), lambda i, n, pt: (i, 0, 0))
    full = lambda a: pl.BlockSpec(a.shape, lambda i, n, pt: (0, 0))
    in_specs = [q_spec, q_spec]
    args = [qa, qb]
    for c in caches:
        in_specs += page_specs(c.shape[1])
        args += [c] * ppb
    in_specs += [full(a) for a in new_t] + [full(slope_rows)]
    args += list(new_t) + [slope_rows]
    vm = lambda *s: pltpu.VMEM(s, F32)
    return pl.pallas_call(
        functools.partial(_sample_body, past_len=past_len, nblk=nblk, ppb=ppb, dec_seq=dec_seq, page=page),
        grid_spec=pltpu.PrefetchScalarGridSpec(
            num_scalar_prefetch=1,
            grid=(nb_, nblk),
            in_specs=in_specs,
            out_specs=(q_spec, q_spec),
            scratch_shapes=[
                pltpu.VMEM((2 * nrow, BRANCH_W), BF16), pltpu.VMEM((2 * nrow, BRANCH_W), BF16),
                vm(MOBA_TOPK, nrow, LANES), vm(MOBA_TOPK, nrow, LANES), vm(MOBA_TOPK, nrow, LANES),
                vm(MOBA_TOPK, nrow, BRANCH_W),
                vm(nrow, LANES), vm(nrow, LANES), vm(nrow, BRANCH_W), vm(N_HEADS, LANES)],
        ),
        out_shape=(jax.ShapeDtypeStruct(qa.shape, F32), jax.ShapeDtypeStruct(qa.shape, F32)),
        compiler_params=_cparams("arbitrary", "arbitrary"),
        name="sample_attn",
    )(page_table, *args)


def _prep_weights(w_in, b_f, w_pa, w_pb, w_o, ln_g, ln_b):
    d = w_in.shape[0]
    w = BRANCH_W
    cols = {}
    off = 0
    for name, size in (("qa", w), ("ka", w), ("va", w), ("za", w), ("qb", w), ("kb", w), ("vb", w), ("zb", w),
                       ("f", N_HEADS), ("ga", d), ("gb", d)):
        cols[name] = w_in[:, off:off + size]
        off += size
    w_rows = jnp.concatenate([cols[k] for k in ("qa", "ka", "va", "qb", "kb", "vb")], axis=1).astype(BF16)
    wf = cols["f"]
    return {
        "w_rows": w_rows,
        "w_t": w_rows.T,
        "w_kk": jnp.concatenate([cols["ka"], cols["kb"]], axis=1).astype(BF16),
        "wft": jnp.pad(wf.T, ((0, N_HEADS), (0, 0))),
        "wfr": jnp.pad(wf, ((0, 0), (0, LANES - N_HEADS))),
        "bf_col": jnp.broadcast_to(b_f[:, None], (N_HEADS, LANES)),
        "bf_row": jnp.pad(b_f[None, :], ((0, 0), (0, LANES - N_HEADS))),
        "w_z": jnp.concatenate([cols["za"], cols["zb"]], axis=1).astype(BF16),
        "w_g": jnp.concatenate([cols["ga"], cols["gb"]], axis=1).astype(BF16),
        "w_pa": w_pa.astype(BF16), "w_pb": w_pb.astype(BF16), "w_o": w_o.astype(BF16),
        "ln_g": ln_g[None, :], "ln_b": ln_b[None, :],
    }


def _alibi_slopes():
    return 2.0 ** (-8.0 * (jnp.arange(N_HEADS, dtype=F32) + 1.0) / N_HEADS)


def _heads_last(a_t, b, t):
    return a_t.reshape(b, N_HEADS, D_HEAD, t).transpose(0, 3, 1, 2)


def kernel(x_prompt, x_sample, cache_moba_k, cache_moba_v, cache_fox_k, cache_fox_v, cache_fox_logf, page_table,
           c_prompt, c_sample, w_ada, b_ada, w_in, b_f, w_pa, w_pb, w_o, ln_g, ln_b):
    depth = w_ada.shape[0]
    bp, t, d = x_prompt.shape
    bs, dec_seq, _ = x_sample.shape
    n_pool, page = cache_moba_k.shape[1], cache_moba_k.shape[2]
    past_len = page_table.shape[1] * page
    assert t % MOBA_BLOCK == 0 and past_len % MOBA_BLOCK == 0 and MOBA_BLOCK % page == 0
    assert dec_seq & (dec_seq - 1) == 0 and (bs * dec_seq) % 8 == 0
    alpha = (2.0 * depth) ** 0.25
    tq = tk = MOBA_BLOCK
    tm = 512 if t % 512 == 0 else MOBA_BLOCK

    slopes = _alibi_slopes()
    slope_rows = jnp.broadcast_to(jnp.tile(slopes, dec_seq)[:, None], (dec_seq * N_HEADS, LANES))

    x_p = x_prompt
    x_s = x_sample.reshape(1, bs * dec_seq, d)
    outs = [[] for _ in range(10)]
    for l in range(depth):
        wp = _prep_weights(w_in[l], b_f[l], w_pa[l], w_pb[l], w_o[l], ln_g[l], ln_b[l])
        mod = _ada(jnp.concatenate([c_prompt, c_sample], axis=0), w_ada[l], b_ada[l])
        shift, scale, gate = (mod[:, i * d:(i + 1) * d] for i in range(3))
        mod_p = [m[:bp, None, :] for m in (scale, shift, gate)]
        mod_s = [jnp.repeat(m[bp:], dec_seq, axis=0)[None] for m in (scale, shift, gate)]

        (kat, vat, kbt, vbt, qat, qbt, vag, vbg, kar, kbr, lft, ct, fkaug, kmean) = _project_prompt(
            x_p, mod_p[0], mod_p[1], wp, tm, tk)
        nblk_tile = tm // MOBA_BLOCK
        kmean = kmean[:, :, :nblk_tile, :].reshape(bp, t // MOBA_BLOCK, BRANCH_W)
        oa = _moba_prompt(slopes, qat, kar, vag, kmean, tq, tk)
        ob = _fox_prompt(qbt, kbr, fkaug, vbg, ct, tq, tk)
        x_p = _combine(x_p, mod_p[0], mod_p[1], mod_p[2], oa, ob, wp, alpha, tm)
        for i, a in enumerate((kat, vat, kbt, vbt)):
            outs[i].append(_heads_last(a, bp, t))
        outs[4].append(lft.transpose(0, 2, 1))

        n_s = bs * dec_seq
        (qa_s, ka_s, va_s, qb_s, kb_s, vb_s, lfr_s, kat_s, vat_s, kbt_s, vbt_s, lft_s) = _project_sample(
            x_s[0], mod_s[0][0], mod_s[1][0], wp)
        feat_major = lambda c: c[l].transpose(0, 2, 3, 1).reshape(n_pool, BRANCH_W, page)
        caches = [feat_major(c) for c in (cache_moba_k, cache_moba_v, cache_fox_k, cache_fox_v)]
        caches.append(cache_fox_logf[l].transpose(0, 2, 1))
        oa_s, ob_s = _sample_attention(page_table, qa_s.reshape(bs, dec_seq, BRANCH_W),
                                       qb_s.reshape(bs, dec_seq, BRANCH_W), caches,
                                       (kat_s, vat_s, kbt_s, vbt_s, lft_s), slope_rows, past_len, page)
        x_s = _combine(x_s, mod_s[0], mod_s[1], mod_s[2], oa_s.reshape(1, n_s, BRANCH_W),
                       ob_s.reshape(1, n_s, BRANCH_W), wp, alpha, n_s)
        for i, a in enumerate((ka_s, va_s, kb_s, vb_s)):
            outs[5 + i].append(a.reshape(bs, dec_seq, N_HEADS, D_HEAD))
        outs[9].append(lfr_s[:, :N_HEADS].reshape(bs, dec_seq, N_HEADS))

    return (x_p, x_s.reshape(bs, dec_seq, d)) + tuple(jnp.stack(o) for o in outs)
```

```python
import functools

import jax
import jax.numpy as jnp
from jax import lax
from jax.experimental import pallas as pl
from jax.experimental.pallas import tpu as pltpu

F32 = jnp.float32
BF16 = jnp.bfloat16

N_HEADS = 8
D_HEAD = 64
BRANCH_W = N_HEADS * D_HEAD
MOBA_BLOCK = 256
MOBA_TOPK = 3
LN_EPS = 1e-5
LANES = 128
V_ROWS = 80
VMEM_LIMIT = 56 * 1024 * 1024
NEG_INF = float("-inf")
SAMPLE_BLOCKS_PER_STEP = 4
HEADS_PER_STEP = 8


def _cparams(*sem):
    return pltpu.CompilerParams(dimension_semantics=sem, vmem_limit_bytes=VMEM_LIMIT)


def _split3(x):
    hi = x.astype(BF16).astype(F32)
    r = x - hi
    mid = r.astype(BF16).astype(F32)
    lo = (r - mid).astype(BF16).astype(F32)
    return hi, mid, lo


def _lane_cumsum(x, seg=None):
    n = x.shape[-1]
    lane = lax.broadcasted_iota(jnp.int32, x.shape, x.ndim - 1)
    pos = lane if seg is None else lane & (seg - 1)
    limit = n if seg is None else seg
    s = 1
    while s < limit:
        x = x + jnp.where(pos >= s, pltpu.roll(x, s, axis=x.ndim - 1), 0.0)
        s *= 2
    return x


def _log_sigmoid(x):
    return jnp.minimum(x, 0.0) - jnp.log(1.0 + jnp.exp(-jnp.abs(x)))


def _sigmoid(x):
    return 1.0 / (1.0 + jnp.exp(-x))


def _dot(a, b):
    return jnp.dot(a, b, preferred_element_type=F32)


def _dot_nt(a, b):
    return lax.dot_general(a, b, (((1,), (1,)), ((), ())), preferred_element_type=F32)


def _ada_body(c_ref, w_ref, b_ref, o_ref):
    o_ref[...] = lax.dot_general(c_ref[...], w_ref[...], (((1,), (0,)), ((), ())),
                                 precision=lax.Precision.HIGHEST,
                                 preferred_element_type=F32) + b_ref[...]


def _ada(c_all, w_ada, b_ada):
    n, d = c_all.shape
    d3 = w_ada.shape[1]
    tn = 1024
    return pl.pallas_call(
        _ada_body,
        grid=(d3 // tn,),
        in_specs=[pl.BlockSpec((n, d), lambda j: (0, 0)),
                  pl.BlockSpec((d, tn), lambda j: (0, j)),
                  pl.BlockSpec((1, tn), lambda j: (0, j))],
        out_specs=pl.BlockSpec((n, tn), lambda j: (0, j)),
        out_shape=jax.ShapeDtypeStruct((n, d3), F32),
        compiler_params=_cparams("arbitrary"),
        name="ada",
    )(c_all, w_ada, b_ada.reshape(1, d3))


def _split2(x):
    hi = x.astype(BF16)
    return hi, (x - hi.astype(F32)).astype(BF16)


def _forget_logits_t(wft_ref, hb, hl):
    w_hi, w_lo = _split2(wft_ref[...])
    return (_dot_nt(w_hi, hb) + _dot_nt(w_hi, hl) + _dot_nt(w_lo, hb))[:N_HEADS]


def _proj_body(x_ref, sc_ref, sh_ref, wkk_ref, wt_ref, wft_ref, bf_ref,
               kat_ref, vat_ref, kbt_ref, vbt_ref, qat_ref, qbt_ref, vag_ref, vbg_ref,
               kar_ref, kbr_ref, lft_ref, ct_ref, fkaug_ref, kmean_ref, carry_ref, *, tm, tk):
    t = pl.program_id(1)

    @pl.when(t == 0)
    def _():
        carry_ref[...] = jnp.zeros_like(carry_ref)

    h32 = x_ref[0] * (1.0 + sc_ref[0]) + sh_ref[0]
    hb = h32.astype(BF16)
    hl = (h32 - hb.astype(F32)).astype(BF16)

    r1 = _dot(hb, wkk_ref[...])
    kar_ref[0] = r1[:, :BRANCH_W].astype(BF16)
    kbr_ref[0] = r1[:, BRANCH_W:].astype(BF16)
    nblk = tm // MOBA_BLOCK
    km = jnp.mean(r1[:, :BRANCH_W].reshape(nblk, MOBA_BLOCK, BRANCH_W), axis=1)
    kmean_ref[0, 0] = jnp.zeros(kmean_ref.shape[2:], F32)
    kmean_ref[0, 0, 0:nblk, :] = km

    ones_rows = jnp.where(lax.broadcasted_iota(jnp.int32, (V_ROWS - D_HEAD, tk), 0) == 0, 1.0, 0.0).astype(BF16)

    def group(g):
        return _dot_nt(wt_ref[g * BRANCH_W:(g + 1) * BRANCH_W, :], hb)

    def store_v(r, vt_ref, vg_ref):
        vt_ref[0] = r
        rb = r.astype(BF16)
        for j in range(tm // tk):
            for h in range(N_HEADS):
                vg_ref[0, j, h * V_ROWS:h * V_ROWS + D_HEAD, :] = rb[h * D_HEAD:(h + 1) * D_HEAD, j * tk:(j + 1) * tk]
                vg_ref[0, j, h * V_ROWS + D_HEAD:(h + 1) * V_ROWS, :] = ones_rows

    scale = D_HEAD ** -0.5
    qat_ref[0] = (group(0) * scale).astype(BF16)
    kat_ref[0] = group(1)
    store_v(group(2), vat_ref, vag_ref)
    qbt_ref[0] = (group(3) * scale).astype(BF16)
    kbt_ref[0] = group(4)
    store_v(group(5), vbt_ref, vbg_ref)

    lf = _log_sigmoid(_forget_logits_t(wft_ref, hb, hl) + bf_ref[:, 0:1])
    lft_ref[0] = lf
    c = _lane_cumsum(lf) + carry_ref[:, 0:1]
    ct_ref[0] = c
    carry_ref[...] = jnp.broadcast_to(c[:, tm - 1:tm], carry_ref.shape)

    hi, mid, lo = _split3(-c)
    ones_blk = jnp.where(lax.broadcasted_iota(jnp.int32, (N_HEADS, tm), 0) < 3, 1.0, 0.0)
    aug_t = jnp.concatenate([hi, mid, lo, ones_blk, jnp.zeros((LANES - 4 * N_HEADS, tm), F32)], axis=0)
    fkaug_ref[0] = aug_t.T.astype(BF16)


def _project_prompt(x, scale, shift, wp, tm, tk):
    b, t, d = x.shape
    nt = t // tm
    f32o = lambda rows: jax.ShapeDtypeStruct((b, rows, t), F32)
    out_shape = (
        f32o(BRANCH_W), f32o(BRANCH_W), f32o(BRANCH_W), f32o(BRANCH_W),
        jax.ShapeDtypeStruct((b, BRANCH_W, t), BF16), jax.ShapeDtypeStruct((b, BRANCH_W, t), BF16),
        jax.ShapeDtypeStruct((b, t // tk, N_HEADS * V_ROWS, tk), BF16),
        jax.ShapeDtypeStruct((b, t // tk, N_HEADS * V_ROWS, tk), BF16),
        jax.ShapeDtypeStruct((b, t, BRANCH_W), BF16), jax.ShapeDtypeStruct((b, t, BRANCH_W), BF16),
        f32o(N_HEADS), f32o(N_HEADS),
        jax.ShapeDtypeStruct((b, t, LANES), BF16),
        jax.ShapeDtypeStruct((b, nt, 8, BRANCH_W), F32),
    )
    tmap = lambda i, j: (i, 0, j)
    rmap = lambda i, j: (i, j, 0)
    vg_spec = pl.BlockSpec((1, tm // tk, N_HEADS * V_ROWS, tk), lambda i, j: (i, j, 0, 0))
    const = lambda i, j: (0, 0)
    out_specs = (
        pl.BlockSpec((1, BRANCH_W, tm), tmap), pl.BlockSpec((1, BRANCH_W, tm), tmap),
        pl.BlockSpec((1, BRANCH_W, tm), tmap), pl.BlockSpec((1, BRANCH_W, tm), tmap),
        pl.BlockSpec((1, BRANCH_W, tm), tmap), pl.BlockSpec((1, BRANCH_W, tm), tmap),
        vg_spec, vg_spec,
        pl.BlockSpec((1, tm, BRANCH_W), rmap), pl.BlockSpec((1, tm, BRANCH_W), rmap),
        pl.BlockSpec((1, N_HEADS, tm), tmap), pl.BlockSpec((1, N_HEADS, tm), tmap),
        pl.BlockSpec((1, tm, LANES), rmap),
        pl.BlockSpec((1, 1, 8, BRANCH_W), lambda i, j: (i, j, 0, 0)),
    )
    in_specs = [
        pl.BlockSpec((1, tm, d), rmap),
        pl.BlockSpec((1, 1, d), lambda i, j: (i, 0, 0)),
        pl.BlockSpec((1, 1, d), lambda i, j: (i, 0, 0)),
        pl.BlockSpec(wp["w_kk"].shape, const),
        pl.BlockSpec(wp["w_t"].shape, const),
        pl.BlockSpec(wp["wft"].shape, const),
        pl.BlockSpec(wp["bf_col"].shape, const),
    ]
    return pl.pallas_call(
        functools.partial(_proj_body, tm=tm, tk=tk),
        grid=(b, nt),
        in_specs=in_specs,
        out_specs=out_specs,
        out_shape=out_shape,
        scratch_shapes=[pltpu.VMEM((N_HEADS, LANES), F32)],
        compiler_params=_cparams("arbitrary", "arbitrary"),
        name="proj_prompt",
    )(x, scale, shift, wp["w_kk"], wp["w_t"], wp["wft"], wp["bf_col"])


def _masked_q(q_ref, hh, tq):
    pair = hh // 2
    qp = q_ref[0, pair * LANES:(pair + 1) * LANES, :]
    z = jnp.zeros((D_HEAD, tq), BF16)
    return jnp.concatenate([qp[:D_HEAD], z], axis=0) if hh % 2 == 0 else jnp.concatenate([z, qp[D_HEAD:]], axis=0)


def _flash_heads(n_heads, lhs_tile, rhs_ref, v_tile, past_bias, qi, diag_mask, ot_ref):
    def step(kv, state, mask):
        scores = [_dot(lhs_tile(hh // 2, kv), rhs_ref[hh]) for hh in range(n_heads)]
        out = []
        for hh in range(n_heads):
            s = scores[hh]
            if mask is not None:
                s = jnp.where(mask, s, NEG_INF)
            if past_bias is not None and mask is None:
                s = past_bias(hh, kv) + s
            m_tile = jnp.max(s, axis=0, keepdims=True)
            if state is None:
                m_new = m_tile
                acc = _dot(v_tile(hh, kv), jnp.exp(s - m_new).astype(BF16))
            else:
                m_old, acc_old = state[hh]
                m_new = jnp.maximum(m_old, m_tile)
                acc = jnp.exp(m_old - m_new) * acc_old + _dot(v_tile(hh, kv), jnp.exp(s - m_new).astype(BF16))
            out.append((m_new, acc))
        return tuple(out)

    state = step(qi, None, diag_mask)
    state = lax.fori_loop(0, qi, lambda kv, st: step(kv, st, None), state)
    for hh in range(n_heads):
        acc = state[hh][1]
        ot_ref[hh * D_HEAD:(hh + 1) * D_HEAD, :] = acc[:D_HEAD] / acc[D_HEAD:D_HEAD + 1]


def _attention_scratch(hg, tq):
    return [pltpu.VMEM((hg, 2 * LANES, tq), BF16), pltpu.VMEM((hg * D_HEAD, tq), F32)]


def _fox_body(q_ref, k_ref, ka_ref, v_ref, c_ref, o_ref, rhs_ref, ot_ref, *, tq, tk, hg):
    qi = pl.program_id(2)
    rowi = lax.broadcasted_iota(jnp.int32, (LANES, tq), 0)
    causal = (lax.broadcasted_iota(jnp.int32, (tk, tq), 0) <= lax.broadcasted_iota(jnp.int32, (tk, tq), 1))
    for hh in range(hg):
        h = pl.program_id(1) * hg + hh
        hi, mid, lo = _split3(c_ref[0, pl.ds(h, 1), :])
        aug = jnp.where((rowi < 3 * N_HEADS) & ((rowi & (N_HEADS - 1)) == h), 1.0, 0.0)
        aug = jnp.where(rowi == 3 * N_HEADS, hi, aug)
        aug = jnp.where(rowi == 3 * N_HEADS + 1, mid, aug)
        aug = jnp.where(rowi == 3 * N_HEADS + 2, lo, aug)
        rhs_ref[hh, 0:LANES, :] = _masked_q(q_ref, hh, tq)
        rhs_ref[hh, LANES:2 * LANES, :] = aug.astype(BF16)

    def lhs_tile(pair, kv):
        off = pl.multiple_of(kv * tk, tk)
        return jnp.concatenate([k_ref[0, pl.ds(off, tk), pair * LANES:(pair + 1) * LANES],
                                ka_ref[0, pl.ds(off, tk), :]], axis=1)

    def v_tile(hh, kv):
        return v_ref[0, kv, hh * V_ROWS:(hh + 1) * V_ROWS, :]

    _flash_heads(hg, lhs_tile, rhs_ref, v_tile, None, qi, causal, ot_ref)
    o_ref[0] = ot_ref[...].T.astype(o_ref.dtype)


def _fox_prompt(qt, kr, kaug, vg, ct, tq, tk, hg):
    b, _, t = qt.shape
    nk = t // tk
    return pl.pallas_call(
        functools.partial(_fox_body, tq=tq, tk=tk, hg=hg),
        grid=(b, N_HEADS // hg, t // tq),
        in_specs=[
            pl.BlockSpec((1, hg * D_HEAD, tq), lambda i, g, j: (i, g, j)),
            pl.BlockSpec((1, t, hg * D_HEAD), lambda i, g, j: (i, 0, g)),
            pl.BlockSpec((1, t, LANES), lambda i, g, j: (i, 0, 0)),
            pl.BlockSpec((1, nk, hg * V_ROWS, tk), lambda i, g, j: (i, 0, g, 0)),
            pl.BlockSpec((1, N_HEADS, tq), lambda i, g, j: (i, 0, j)),
        ],
        out_specs=pl.BlockSpec((1, tq, hg * D_HEAD), lambda i, g, j: (i, j, g)),
        out_shape=jax.ShapeDtypeStruct((b, t, BRANCH_W), BF16),
        scratch_shapes=_attention_scratch(hg, tq),
        compiler_params=_cparams("arbitrary", "arbitrary", "arbitrary"),
        name="fox_prompt",
    )(qt, kr, kaug, vg, ct)


def _moba_body(sl_ref, q_ref, k_ref, v_ref, km_ref, o_ref, rhs_ref, ot_ref, sel_ref, pa_ref, *, tq, tk, nb, hg):
    qi = pl.program_id(2)

    @pl.when((pl.program_id(0) == 0) & (pl.program_id(1) == 0) & (qi == 0))
    def _():
        shape = pa_ref.shape
        pos = lax.broadcasted_iota(jnp.int32, shape, 0).astype(F32)
        col = lax.broadcasted_iota(jnp.int32, shape, 1)
        p_hi = pos.astype(BF16).astype(F32)
        pa_ref[...] = jnp.where(col < 3, p_hi, jnp.where(col < 6, pos - p_hi, jnp.where(col < 9, 1.0, 0.0))).astype(BF16)

    rowi = lax.broadcasted_iota(jnp.int32, (LANES, tq), 0)
    causal = (lax.broadcasted_iota(jnp.int32, (tk, tq), 0) <= lax.broadcasted_iota(jnp.int32, (tk, tq), 1))
    blk = lax.broadcasted_iota(jnp.int32, (nb, tq), 0)
    tpos = (qi * tq + lax.broadcasted_iota(jnp.int32, (1, tq), 1)).astype(F32)
    for hh in range(hg):
        h = pl.program_id(1) * hg + hh
        qm = _masked_q(q_ref, hh, tq)

        pair = hh // 2
        km_hi, km_lo = _split2(km_ref[0][:, pair * LANES:(pair + 1) * LANES])
        sc = _dot(km_hi, qm) + _dot(km_lo, qm)
        rank = jnp.zeros((nb, tq), F32)
        for mm in range(nb):
            row = sc[mm:mm + 1, :]
            live = (mm < qi).astype(F32)
            rank = rank + jnp.where(blk > mm, jnp.where(row >= sc, live, 0.0), jnp.where(row > sc, live, 0.0))
        chosen = (blk < qi) & (rank < MOBA_TOPK)
        sel_ref[hh] = jnp.where(chosen, 0.0, NEG_INF)

        slope = jnp.full((1, tq), sl_ref[h], F32)
        s_hi, s_mid, s_lo = _split3(slope)
        a_hi, a_mid, a_lo = _split3(-(slope * tpos))
        aug = jnp.where((rowi == 0) | (rowi == 3), s_hi, 0.0)
        aug = jnp.where((rowi == 1) | (rowi == 4), s_mid, aug)
        aug = jnp.where((rowi == 2) | (rowi == 5), s_lo, aug)
        aug = jnp.where(rowi == 6, a_hi, aug)
        aug = jnp.where(rowi == 7, a_mid, aug)
        aug = jnp.where(rowi == 8, a_lo, aug)
        rhs_ref[hh, 0:LANES, :] = qm
        rhs_ref[hh, LANES:2 * LANES, :] = aug.astype(BF16)

    def lhs_tile(pair, kv):
        off = pl.multiple_of(kv * tk, tk)
        return jnp.concatenate([k_ref[0, pl.ds(off, tk), pair * LANES:(pair + 1) * LANES],
                                pa_ref[pl.ds(off, tk), :]], axis=1)

    def v_tile(hh, kv):
        return v_ref[0, kv, hh * V_ROWS:(hh + 1) * V_ROWS, :]

    def past_bias(hh, kv):
        return sel_ref[hh, pl.ds(kv, 1), :]

    _flash_heads(hg, lhs_tile, rhs_ref, v_tile, past_bias, qi, causal, ot_ref)
    o_ref[0] = ot_ref[...].T.astype(o_ref.dtype)


def _moba_prompt(slopes, qt, kr, vg, kmean, tq, tk, hg):
    b, _, t = qt.shape
    nk = t // tk
    nb = kmean.shape[1]
    return pl.pallas_call(
        functools.partial(_moba_body, tq=tq, tk=tk, nb=nb, hg=hg),
        grid=(b, N_HEADS // hg, t // tq),
        in_specs=[
            pl.BlockSpec(memory_space=pltpu.SMEM),
            pl.BlockSpec((1, hg * D_HEAD, tq), lambda i, g, j: (i, g, j)),
            pl.BlockSpec((1, t, hg * D_HEAD), lambda i, g, j: (i, 0, g)),
            pl.BlockSpec((1, nk, hg * V_ROWS, tk), lambda i, g, j: (i, 0, g, 0)),
            pl.BlockSpec((1, nb, hg * D_HEAD), lambda i, g, j: (i, 0, g)),
        ],
        out_specs=pl.BlockSpec((1, tq, hg * D_HEAD), lambda i, g, j: (i, j, g)),
        scratch_shapes=_attention_scratch(hg, tq) + [pltpu.VMEM((hg, nb, tq), F32), pltpu.VMEM((t, LANES), BF16)],
        out_shape=jax.ShapeDtypeStruct((b, t, BRANCH_W), BF16),
        compiler_params=_cparams("arbitrary", "arbitrary", "arbitrary"),
        name="moba_prompt",
    )(slopes, qt, kr, vg, kmean)


def _combine_body(x_ref, sc_ref, sh_ref, g_ref, oa_ref, ob_ref, wz_ref, wg_ref, wpa_ref, wpb_ref, wo_ref,
                  lng_ref, lnb_ref, y_ref, *, alpha):
    x = x_ref[0]
    hb = (x * (1.0 + sc_ref[0]) + sh_ref[0]).astype(BF16)
    z = _dot(hb, wz_ref[...])
    g = _dot(hb, wg_ref[...])
    za, zb = z[:, :BRANCH_W], z[:, BRANCH_W:]
    d = x.shape[-1]
    ua = (oa_ref[0].astype(F32) * (za * _sigmoid(za))).astype(BF16)
    ub = (ob_ref[0].astype(F32) * (zb * _sigmoid(zb))).astype(BF16)
    ya = _dot(ua, wpa_ref[...])
    yb = _dot(ub, wpb_ref[...])
    mix = (_sigmoid(g[:, :d]) * ya + _sigmoid(g[:, d:]) * yb).astype(BF16)
    s = _dot(mix, wo_ref[...])
    r = alpha * x + g_ref[0] * s
    mu = jnp.mean(r, axis=-1, keepdims=True)
    dev = r - mu
    var = jnp.mean(dev * dev, axis=-1, keepdims=True)
    y_ref[0] = dev * lax.rsqrt(var + LN_EPS) * lng_ref[...] + lnb_ref[...]


def _combine(x, scale, shift, gate, oa, ob, wp, alpha, tm):
    b, t, d = x.shape
    tmod = scale.shape[1]
    rmap = lambda i, j: (i, j, 0)
    mod_spec = pl.BlockSpec((1, 1, d), lambda i, j: (i, 0, 0)) if tmod == 1 else pl.BlockSpec((1, tm, d), rmap)
    const = lambda i, j: (0, 0)
    names = ("w_z", "w_g", "w_pa", "w_pb", "w_o", "ln_g", "ln_b")
    return pl.pallas_call(
        functools.partial(_combine_body, alpha=alpha),
        grid=(b, t // tm),
        in_specs=[pl.BlockSpec((1, tm, d), rmap), mod_spec, mod_spec, mod_spec,
                  pl.BlockSpec((1, tm, BRANCH_W), rmap), pl.BlockSpec((1, tm, BRANCH_W), rmap)]
                 + [pl.BlockSpec(wp[k].shape, const) for k in names],
        out_specs=pl.BlockSpec((1, tm, d), rmap),
        out_shape=jax.ShapeDtypeStruct((b, t, d), F32),
        compiler_params=_cparams("arbitrary", "arbitrary"),
        name="combine",
    )(x, scale, shift, gate, oa, ob, *[wp[k] for k in names])


def _proj_sample_body(x_ref, sc_ref, sh_ref, wr_ref, wt_ref, wft_ref, wfr_ref, bf_ref, bfr_ref,
                      qa_ref, ka_ref, va_ref, qb_ref, kb_ref, vb_ref, lfr_ref,
                      kat_ref, vat_ref, kbt_ref, vbt_ref, lft_ref):
    h32 = x_ref[...] * (1.0 + sc_ref[...]) + sh_ref[...]
    hb = h32.astype(BF16)
    hl = (h32 - hb.astype(F32)).astype(BF16)
    r = _dot(hb, wr_ref[...])
    scale = D_HEAD ** -0.5
    w = BRANCH_W
    qa_ref[...] = r[:, 0:w] * scale
    ka_ref[...] = r[:, w:2 * w]
    va_ref[...] = r[:, 2 * w:3 * w]
    qb_ref[...] = r[:, 3 * w:4 * w] * scale
    kb_ref[...] = r[:, 4 * w:5 * w]
    vb_ref[...] = r[:, 5 * w:6 * w]
    wr_hi, wr_lo = _split2(wfr_ref[...])
    flr = _dot(hb, wr_hi) + _dot(hl, wr_hi) + _dot(hb, wr_lo)
    lfr_ref[...] = _log_sigmoid(flr + bfr_ref[...])
    for g, ref in ((1, kat_ref), (2, vat_ref), (4, kbt_ref), (5, vbt_ref)):
        ref[...] = _dot_nt(wt_ref[g * w:(g + 1) * w, :], hb)
    lft_ref[...] = _log_sigmoid(_forget_logits_t(wft_ref, hb, hl) + bf_ref[:, 0:1])


def _project_sample(x, scale, shift, wp):
    n, d = x.shape
    row = jax.ShapeDtypeStruct((n, BRANCH_W), F32)
    col = jax.ShapeDtypeStruct((BRANCH_W, n), F32)
    out_shape = (row,) * 6 + (jax.ShapeDtypeStruct((n, LANES), F32),) + (col,) * 4 + (
        jax.ShapeDtypeStruct((N_HEADS, n), F32),)
    args = (x, scale, shift, wp["w_rows"], wp["w_t"], wp["wft"], wp["wfr"], wp["bf_col"], wp["bf_row"])
    return pl.pallas_call(
        _proj_sample_body,
        out_shape=out_shape,
        compiler_params=pltpu.CompilerParams(vmem_limit_bytes=VMEM_LIMIT),
        name="proj_sample",
    )(*args)


def _shift_right(n):
    return n.bit_length() - 1


def _sample_body(pt_ref, qa_ref, qb_ref, *refs, past_len, nstep, nbs, ppb, dec_seq, page):
    pps = nbs * ppb
    caches = refs[:5 * pps]
    mk, mv, fk, fv, fl = (caches[i * pps:(i + 1) * pps] for i in range(5))
    kan_ref, van_ref, kbn_ref, vbn_ref, lfn_ref, slope_ref, oa_ref, ob_ref = refs[5 * pps:5 * pps + 8]
    qda_ref, qdb_ref, ssc_ref, sm_ref, sl_ref, so_ref, fm_ref, fls_ref, facc_ref, fcar_ref = refs[5 * pps + 8:]
    del pt_ref
    b = pl.program_id(0)
    n = pl.program_id(1)
    nrow = dec_seq * N_HEADS
    nkey = pps * page
    ntok = kan_ref.shape[1]
    qsh = _shift_right(N_HEADS)

    def rows_iota(shape):
        return lax.broadcasted_iota(jnp.int32, shape, 0)

    def lanes_iota(shape):
        return lax.broadcasted_iota(jnp.int32, shape, 1)

    head_mask = (lanes_iota((nrow, BRANCH_W)) >> _shift_right(D_HEAD)) == (rows_iota((nrow, BRANCH_W)) & (N_HEADS - 1))

    def block_diag_q(q_ref):
        q = q_ref[0]
        q32 = jnp.concatenate([jnp.broadcast_to(q[i:i + 1, :], (N_HEADS, BRANCH_W)) for i in range(dec_seq)], axis=0)
        q32 = jnp.where(head_mask, q32, 0.0)
        hi = q32.astype(BF16)
        lo = (q32 - hi.astype(F32)).astype(BF16)
        return jnp.concatenate([hi, lo], axis=0)

    @pl.when(n == 0)
    def _():
        qda_ref[...] = block_diag_q(qa_ref)
        qdb_ref[...] = block_diag_q(qb_ref)
        ssc_ref[...] = jnp.full(ssc_ref.shape, NEG_INF, F32)
        sm_ref[...] = jnp.full(sm_ref.shape, NEG_INF, F32)
        sl_ref[...] = jnp.zeros(sl_ref.shape, F32)
        so_ref[...] = jnp.zeros(so_ref.shape, F32)
        fm_ref[...] = jnp.full(fm_ref.shape, NEG_INF, F32)
        fls_ref[...] = jnp.zeros(fls_ref.shape, F32)
        facc_ref[...] = jnp.zeros(facc_ref.shape, F32)
        fcar_ref[...] = jnp.zeros(fcar_ref.shape, F32)

    def qk(qd_ref, kt):
        s2 = _dot(qd_ref[...], kt)
        return s2[:nrow] + s2[nrow:]

    def bcast(col):
        return jnp.broadcast_to(col, (nrow, LANES))

    def softmax_part(lg):
        m = jnp.max(lg, axis=1, keepdims=True)
        p = jnp.exp(lg - m).astype(BF16)
        return m, p, jnp.sum(p.astype(F32), axis=1, keepdims=True)

    slope = slope_ref[:, 0:1]
    qrow = rows_iota((nrow, MOBA_BLOCK)) >> qsh

    def block_t(refs, j):
        return jnp.concatenate([r[0] for r in refs[j * ppb:(j + 1) * ppb]], axis=1).astype(BF16)

    s_moba = [qk(qda_ref, block_t(mk, j)) for j in range(nbs)]
    s_fox = [qk(qdb_ref, block_t(fk, j)) for j in range(nbs)]

    parts = []
    for j in range(nbs):
        score = jnp.sum(s_moba[j], axis=1, keepdims=True)
        dist = (past_len - (n * nbs + j) * MOBA_BLOCK + qrow - lanes_iota((nrow, MOBA_BLOCK))).astype(F32)
        m_n, p, l_n = softmax_part(s_moba[j] - slope * dist)
        parts.append((score, m_n, l_n, p))

    lf = jnp.concatenate([r[0] for r in fl], axis=1)
    c_blk = _lane_cumsum(lf) + fcar_ref[:, 0:1]
    fcar_ref[...] = jnp.broadcast_to(c_blk[:, nkey - 1:nkey], fcar_ref.shape)
    c_rows = jnp.concatenate([c_blk] * dec_seq, axis=0)
    u = jnp.concatenate(s_fox, axis=1) - c_rows
    m_old = fm_ref[:, 0:1]
    m_new = jnp.maximum(m_old, jnp.max(u, axis=1, keepdims=True))
    alpha = jnp.exp(m_old - m_new)
    p_fox = jnp.exp(u - m_new).astype(BF16)
    fls_ref[...] = bcast(alpha * fls_ref[:, 0:1] + jnp.sum(p_fox.astype(F32), axis=1, keepdims=True))
    fm_ref[...] = bcast(m_new)

    o_moba = [_dot_nt(parts[j][3], block_t(mv, j)) for j in range(nbs)]
    o_fox = _dot_nt(p_fox[:, 0:MOBA_BLOCK], block_t(fv, 0))
    for j in range(1, nbs):
        o_fox = o_fox + _dot_nt(p_fox[:, j * MOBA_BLOCK:(j + 1) * MOBA_BLOCK], block_t(fv, j))
    facc_ref[...] = alpha * facc_ref[...] + o_fox

    state = (ssc_ref, sm_ref, sl_ref, so_ref)
    slots = [[ref[i] for ref in state] for i in range(MOBA_TOPK)]
    for j in range(nbs):
        score, m_n, l_n, _ = parts[j]
        new = (bcast(score), bcast(m_n), bcast(l_n), o_moba[j])
        gt = [score > slots[i][0][:, 0:1] for i in range(MOBA_TOPK)]
        nxt = []
        for i in range(MOBA_TOPK):
            row = []
            for f in range(len(state)):
                keep = jnp.where(gt[i], new[f], slots[i][f])
                row.append(keep if i == 0 else jnp.where(gt[i - 1], slots[i - 1][f], keep))
            nxt.append(row)
        slots = nxt
    for i in range(MOBA_TOPK):
        for f, ref in enumerate(state):
            ref[i] = slots[i][f]

    @pl.when(n == nstep - 1)
    def _():
        lane = lanes_iota((nrow, ntok))
        qr = rows_iota((nrow, ntok)) >> qsh
        tok_q = lane & (dec_seq - 1)
        visible = ((lane >> _shift_right(dec_seq)) == b) & (tok_q <= qr)

        def collapse(o):
            return jnp.sum(jnp.where(head_mask, o, 0.0).reshape(dec_seq, N_HEADS, BRANCH_W), axis=1)

        s_own = qk(qda_ref, kan_ref[...].astype(BF16))
        lg = jnp.where(visible, s_own - slope * (qr - tok_q).astype(F32), NEG_INF)
        m_o, p_o, l_o = softmax_part(lg)
        o_o = _dot_nt(p_o, van_ref[...].astype(BF16))
        ms = [sm_ref[i][:, 0:1] for i in range(MOBA_TOPK)]
        m_tot = m_o
        for mi in ms:
            m_tot = jnp.maximum(m_tot, mi)
        w_o = jnp.exp(m_o - m_tot)
        num = w_o * o_o
        den = w_o * l_o
        for i in range(MOBA_TOPK):
            w_i = jnp.exp(ms[i] - m_tot)
            num = num + w_i * so_ref[i]
            den = den + w_i * sl_ref[i][:, 0:1]
        oa_ref[0] = collapse(num / den)

        c_new = _lane_cumsum(lfn_ref[...], seg=dec_seq)
        c_new_rows = jnp.concatenate([c_new] * dec_seq, axis=0)
        cq = jnp.sum(jnp.where(lane == b * dec_seq + qr, c_new_rows, 0.0), axis=1, keepdims=True)
        c_past = jnp.concatenate([fcar_ref[:, 0:1]] * dec_seq, axis=0)
        m_p = fm_ref[:, 0:1] + c_past + cq
        s_new = qk(qdb_ref, kbn_ref[...].astype(BF16))
        lg = jnp.where(visible, s_new + cq - c_new_rows, NEG_INF)
        m_o, p_o, l_o = softmax_part(lg)
        o_o = _dot_nt(p_o, vbn_ref[...].astype(BF16))
        m_tot = jnp.maximum(m_p, m_o)
        w_p = jnp.exp(m_p - m_tot)
        w_o = jnp.exp(m_o - m_tot)
        num = w_p * facc_ref[...] + w_o * o_o
        den = w_p * fls_ref[:, 0:1] + w_o * l_o
        ob_ref[0] = collapse(num / den)


def _sample_attention(page_table, qa, qb, caches, new_t, slope_rows, past_len, page):
    nb_, dec_seq, _ = qa.shape
    ppb = MOBA_BLOCK // page
    nblk = past_len // MOBA_BLOCK
    nbs = next(c for c in (SAMPLE_BLOCKS_PER_STEP, 2, 1) if nblk % c == 0)
    pps = nbs * ppb
    nrow = dec_seq * N_HEADS

    def page_specs(rows):
        return [pl.BlockSpec((1, rows, page), functools.partial(lambda i, n, pt, j: (pt[i, n * pps + j], 0, 0), j=j))
                for j in range(pps)]

    q_spec = pl.BlockSpec((1, dec_seq, BRANCH_W), lambda i, n, pt: (i, 0, 0))
    full = lambda a: pl.BlockSpec(a.shape, lambda i, n, pt: (0, 0))
    in_specs = [q_spec, q_spec]
    args = [qa, qb]
    for c in caches:
        in_specs += page_specs(c.shape[1])
        args += [c] * pps
    in_specs += [full(a) for a in new_t] + [full(slope_rows)]
    args += list(new_t) + [slope_rows]
    vm = lambda *s: pltpu.VMEM(s, F32)
    return pl.pallas_call(
        functools.partial(_sample_body, past_len=past_len, nstep=nblk // nbs, nbs=nbs, ppb=ppb, dec_seq=dec_seq,
                          page=page),
        grid_spec=pltpu.PrefetchScalarGridSpec(
            num_scalar_prefetch=1,
            grid=(nb_, nblk // nbs),
            in_specs=in_specs,
            out_specs=(q_spec, q_spec),
            scratch_shapes=[
                pltpu.VMEM((2 * nrow, BRANCH_W), BF16), pltpu.VMEM((2 * nrow, BRANCH_W), BF16),
                vm(MOBA_TOPK, nrow, LANES), vm(MOBA_TOPK, nrow, LANES), vm(MOBA_TOPK, nrow, LANES),
                vm(MOBA_TOPK, nrow, BRANCH_W),
                vm(nrow, LANES), vm(nrow, LANES), vm(nrow, BRANCH_W), vm(N_HEADS, LANES)],
        ),
        out_shape=(jax.ShapeDtypeStruct(qa.shape, F32), jax.ShapeDtypeStruct(qa.shape, F32)),
        compiler_params=_cparams("arbitrary", "arbitrary"),
        name="sample_attn",
    )(page_table, *args)


def _prep_weights(w_in, b_f, w_pa, w_pb, w_o, ln_g, ln_b):
    d = w_in.shape[0]
    w = BRANCH_W
    cols = {}
    off = 0
    for name, size in (("qa", w), ("ka", w), ("va", w), ("za", w), ("qb", w), ("kb", w), ("vb", w), ("zb", w),
                       ("f", N_HEADS), ("ga", d), ("gb", d)):
        cols[name] = w_in[:, off:off + size]
        off += size
    w_rows = jnp.concatenate([cols[k] for k in ("qa", "ka", "va", "qb", "kb", "vb")], axis=1).astype(BF16)
    wf = cols["f"]
    return {
        "w_rows": w_rows,
        "w_t": w_rows.T,
        "w_kk": jnp.concatenate([cols["ka"], cols["kb"]], axis=1).astype(BF16),
        "wft": jnp.pad(wf.T, ((0, N_HEADS), (0, 0))),
        "wfr": jnp.pad(wf, ((0, 0), (0, LANES - N_HEADS))),
        "bf_col": jnp.broadcast_to(b_f[:, None], (N_HEADS, LANES)),
        "bf_row": jnp.pad(b_f[None, :], ((0, 0), (0, LANES - N_HEADS))),
        "w_z": jnp.concatenate([cols["za"], cols["zb"]], axis=1).astype(BF16),
        "w_g": jnp.concatenate([cols["ga"], cols["gb"]], axis=1).astype(BF16),
        "w_pa": w_pa.astype(BF16), "w_pb": w_pb.astype(BF16), "w_o": w_o.astype(BF16),
        "ln_g": ln_g[None, :], "ln_b": ln_b[None, :],
    }


def _alibi_slopes():
    return 2.0 ** (-8.0 * (jnp.arange(N_HEADS, dtype=F32) + 1.0) / N_HEADS)


def _heads_last(a_t, b, t):
    return a_t.reshape(b, N_HEADS, D_HEAD, t).transpose(0, 3, 1, 2)


def kernel(x_prompt, x_sample, cache_moba_k, cache_moba_v, cache_fox_k, cache_fox_v, cache_fox_logf, page_table,
           c_prompt, c_sample, w_ada, b_ada, w_in, b_f, w_pa, w_pb, w_o, ln_g, ln_b):
    depth = w_ada.shape[0]
    bp, t, d = x_prompt.shape
    bs, dec_seq, _ = x_sample.shape
    n_pool, page = cache_moba_k.shape[1], cache_moba_k.shape[2]
    past_len = page_table.shape[1] * page
    assert t % MOBA_BLOCK == 0 and past_len % MOBA_BLOCK == 0 and MOBA_BLOCK % page == 0
    assert dec_seq & (dec_seq - 1) == 0 and (bs * dec_seq) % 8 == 0
    alpha = (2.0 * depth) ** 0.25
    tq = tk = MOBA_BLOCK
    tm = 512 if t % 512 == 0 else MOBA_BLOCK

    slopes = _alibi_slopes()
    slope_rows = jnp.broadcast_to(jnp.tile(slopes, dec_seq)[:, None], (dec_seq * N_HEADS, LANES))

    x_p = x_prompt
    x_s = x_sample.reshape(1, bs * dec_seq, d)
    outs = [[] for _ in range(10)]
    for l in range(depth):
        wp = _prep_weights(w_in[l], b_f[l], w_pa[l], w_pb[l], w_o[l], ln_g[l], ln_b[l])
        mod = _ada(jnp.concatenate([c_prompt, c_sample], axis=0), w_ada[l], b_ada[l])
        shift, scale, gate = (mod[:, i * d:(i + 1) * d] for i in range(3))
        mod_p = [m[:bp, None, :] for m in (scale, shift, gate)]
        mod_s = [jnp.repeat(m[bp:], dec_seq, axis=0)[None] for m in (scale, shift, gate)]

        (kat, vat, kbt, vbt, qat, qbt, vag, vbg, kar, kbr, lft, ct, fkaug, kmean) = _project_prompt(
            x_p, mod_p[0], mod_p[1], wp, tm, tk)
        nblk_tile = tm // MOBA_BLOCK
        kmean = kmean[:, :, :nblk_tile, :].reshape(bp, t // MOBA_BLOCK, BRANCH_W)
        oa = _moba_prompt(slopes, qat, kar, vag, kmean, tq, tk, HEADS_PER_STEP)
        ob = _fox_prompt(qbt, kbr, fkaug, vbg, ct, tq, tk, HEADS_PER_STEP)
        x_p = _combine(x_p, mod_p[0], mod_p[1], mod_p[2], oa, ob, wp, alpha, tm)
        for i, a in enumerate((kat, vat, kbt, vbt)):
            outs[i].append(_heads_last(a, bp, t))
        outs[4].append(lft.transpose(0, 2, 1))

        n_s = bs * dec_seq
        (qa_s, ka_s, va_s, qb_s, kb_s, vb_s, lfr_s, kat_s, vat_s, kbt_s, vbt_s, lft_s) = _project_sample(
            x_s[0], mod_s[0][0], mod_s[1][0], wp)
        feat_major = lambda c: c[l].transpose(0, 2, 3, 1).reshape(n_pool, BRANCH_W, page)
        caches = [feat_major(c) for c in (cache_moba_k, cache_moba_v, cache_fox_k, cache_fox_v)]
        caches.append(cache_fox_logf[l].transpose(0, 2, 1))
        oa_s, ob_s = _sample_attention(page_table, qa_s.reshape(bs, dec_seq, BRANCH_W),
                                       qb_s.reshape(bs, dec_seq, BRANCH_W), caches,
                                       (kat_s, vat_s, kbt_s, vbt_s, lft_s), slope_rows, past_len, page)
        x_s = _combine(x_s, mod_s[0], mod_s[1], mod_s[2], oa_s.reshape(1, n_s, BRANCH_W),
                       ob_s.reshape(1, n_s, BRANCH_W), wp, alpha, n_s)
        for i, a in enumerate((ka_s, va_s, kb_s, vb_s)):
            outs[5 + i].append(a.reshape(bs, dec_seq, N_HEADS, D_HEAD))
        outs[9].append(lfr_s[:, :N_HEADS].reshape(bs, dec_seq, N_HEADS))

    return (x_p, x_s.reshape(bs, dec_seq, d)) + tuple(jnp.stack(o) for o in outs)
```

```python
import functools

import jax
import jax.numpy as jnp
from jax import lax
from jax.experimental import pallas as pl
from jax.experimental.pallas import tpu as pltpu

F32 = jnp.float32
BF16 = jnp.bfloat16

N_HEADS = 8
D_HEAD = 64
BRANCH_W = N_HEADS * D_HEAD
MOBA_BLOCK = 256
MOBA_TOPK = 3
LN_EPS = 1e-5
LANES = 128
V_ROWS = 80
VMEM_LIMIT = 56 * 1024 * 1024
NEG_INF = float("-inf")
LOG2E = 1.4426950408889634
SAMPLE_BLOCKS_PER_STEP = 8
HEADS_PER_STEP = 8


def _cparams(*sem):
    return pltpu.CompilerParams(dimension_semantics=sem, vmem_limit_bytes=VMEM_LIMIT)


def _split3(x):
    hi = x.astype(BF16).astype(F32)
    r = x - hi
    mid = r.astype(BF16).astype(F32)
    lo = (r - mid).astype(BF16).astype(F32)
    return hi, mid, lo


def _lane_cumsum(x, seg=None):
    n = x.shape[-1]
    lane = lax.broadcasted_iota(jnp.int32, x.shape, x.ndim - 1)
    pos = lane if seg is None else lane & (seg - 1)
    limit = n if seg is None else seg
    s = 1
    while s < limit:
        x = x + jnp.where(pos >= s, pltpu.roll(x, s, axis=x.ndim - 1), 0.0)
        s *= 2
    return x


def _log_sigmoid(x):
    return jnp.minimum(x, 0.0) - jnp.log(1.0 + jnp.exp(-jnp.abs(x)))


def _sigmoid(x):
    return 1.0 / (1.0 + jnp.exp(-x))


def _dot(a, b):
    return jnp.dot(a, b, preferred_element_type=F32)


def _dot_nt(a, b):
    return lax.dot_general(a, b, (((1,), (1,)), ((), ())), preferred_element_type=F32)


def _ada_body(c_ref, w_ref, b_ref, o_ref):
    o_ref[...] = lax.dot_general(c_ref[...], w_ref[...], (((1,), (0,)), ((), ())),
                                 precision=lax.Precision.HIGHEST,
                                 preferred_element_type=F32) + b_ref[...]


def _ada(c_all, w_ada, b_ada):
    n, d = c_all.shape
    d3 = w_ada.shape[1]
    tn = 1024
    return pl.pallas_call(
        _ada_body,
        grid=(d3 // tn,),
        in_specs=[pl.BlockSpec((n, d), lambda j: (0, 0)),
                  pl.BlockSpec((d, tn), lambda j: (0, j)),
                  pl.BlockSpec((1, tn), lambda j: (0, j))],
        out_specs=pl.BlockSpec((n, tn), lambda j: (0, j)),
        out_shape=jax.ShapeDtypeStruct((n, d3), F32),
        compiler_params=_cparams("arbitrary"),
        name="ada",
    )(c_all, w_ada, b_ada.reshape(1, d3))


def _split2(x):
    hi = x.astype(BF16)
    return hi, (x - hi.astype(F32)).astype(BF16)


def _forget_logits_t(wft_ref, hb, hl):
    w_hi, w_lo = _split2(wft_ref[...])
    return (_dot_nt(w_hi, hb) + _dot_nt(w_hi, hl) + _dot_nt(w_lo, hb))[:N_HEADS]


def _proj_body(x_ref, sc_ref, sh_ref, wt_ref, wft_ref, bf_ref,
               kat_ref, vat_ref, kbt_ref, vbt_ref, qat_ref, qbt_ref, vag_ref, vbg_ref,
               kar_ref, kbr_ref, lft_ref, ct_ref, fkaug_ref, kmean_ref, carry_ref, *, tm, tk):
    t = pl.program_id(1)

    @pl.when(t == 0)
    def _():
        carry_ref[...] = jnp.zeros_like(carry_ref)

    h32 = x_ref[0] * (1.0 + sc_ref[0]) + sh_ref[0]
    hb = h32.astype(BF16)
    hl = (h32 - hb.astype(F32)).astype(BF16)

    ones_rows = jnp.where(lax.broadcasted_iota(jnp.int32, (V_ROWS - D_HEAD, tk), 0) == 0, 1.0, 0.0).astype(BF16)

    r = _dot_nt(wt_ref[...], hb)

    def group(g):
        return r[g * BRANCH_W:(g + 1) * BRANCH_W, :]

    def store_v(r, vt_ref, vg_ref):
        vt_ref[0] = r
        rb = r.astype(BF16)
        for j in range(tm // tk):
            for h in range(N_HEADS):
                vg_ref[0, j, h * V_ROWS:h * V_ROWS + D_HEAD, :] = rb[h * D_HEAD:(h + 1) * D_HEAD, j * tk:(j + 1) * tk]
                vg_ref[0, j, h * V_ROWS + D_HEAD:(h + 1) * V_ROWS, :] = ones_rows

    scale = D_HEAD ** -0.5 * LOG2E
    qat_ref[0] = (group(0) * scale).astype(BF16)
    ka_t = group(1)
    kat_ref[0] = ka_t
    ka_r = ka_t.T
    kar_ref[0] = ka_r.astype(BF16)
    nblk = tm // MOBA_BLOCK
    kmean_ref[0, 0] = jnp.zeros(kmean_ref.shape[2:], F32)
    kmean_ref[0, 0, 0:nblk, :] = jnp.mean(ka_r.reshape(nblk, MOBA_BLOCK, BRANCH_W), axis=1)
    store_v(group(2), vat_ref, vag_ref)
    qbt_ref[0] = (group(3) * scale).astype(BF16)
    kb_t = group(4)
    kbt_ref[0] = kb_t
    kbr_ref[0] = kb_t.T.astype(BF16)
    store_v(group(5), vbt_ref, vbg_ref)

    lf = _log_sigmoid(_forget_logits_t(wft_ref, hb, hl) + bf_ref[:, 0:1])
    lft_ref[0] = lf
    c = _lane_cumsum(lf) + carry_ref[:, 0:1]
    ct_ref[0] = c
    carry_ref[...] = jnp.broadcast_to(c[:, tm - 1:tm], carry_ref.shape)

    hi, mid, lo = _split3(-c * LOG2E)
    ones_blk = jnp.where(lax.broadcasted_iota(jnp.int32, (N_HEADS, tm), 0) < 3, 1.0, 0.0)
    aug_t = jnp.concatenate([hi, mid, lo, ones_blk, jnp.zeros((LANES - 4 * N_HEADS, tm), F32)], axis=0)
    fkaug_ref[0] = aug_t.T.astype(BF16)


def _project_prompt(x, scale, shift, wp, tm, tk):
    b, t, d = x.shape
    nt = t // tm
    f32o = lambda rows: jax.ShapeDtypeStruct((b, rows, t), F32)
    out_shape = (
        f32o(BRANCH_W), f32o(BRANCH_W), f32o(BRANCH_W), f32o(BRANCH_W),
        jax.ShapeDtypeStruct((b, BRANCH_W, t), BF16), jax.ShapeDtypeStruct((b, BRANCH_W, t), BF16),
        jax.ShapeDtypeStruct((b, t // tk, N_HEADS * V_ROWS, tk), BF16),
        jax.ShapeDtypeStruct((b, t // tk, N_HEADS * V_ROWS, tk), BF16),
        jax.ShapeDtypeStruct((b, t, BRANCH_W), BF16), jax.ShapeDtypeStruct((b, t, BRANCH_W), BF16),
        f32o(N_HEADS), f32o(N_HEADS),
        jax.ShapeDtypeStruct((b, t, LANES), BF16),
        jax.ShapeDtypeStruct((b, nt, 8, BRANCH_W), F32),
    )
    tmap = lambda i, j: (i, 0, j)
    rmap = lambda i, j: (i, j, 0)
    vg_spec = pl.BlockSpec((1, tm // tk, N_HEADS * V_ROWS, tk), lambda i, j: (i, j, 0, 0))
    const = lambda i, j: (0, 0)
    out_specs = (
        pl.BlockSpec((1, BRANCH_W, tm), tmap), pl.BlockSpec((1, BRANCH_W, tm), tmap),
        pl.BlockSpec((1, BRANCH_W, tm), tmap), pl.BlockSpec((1, BRANCH_W, tm), tmap),
        pl.BlockSpec((1, BRANCH_W, tm), tmap), pl.BlockSpec((1, BRANCH_W, tm), tmap),
        vg_spec, vg_spec,
        pl.BlockSpec((1, tm, BRANCH_W), rmap), pl.BlockSpec((1, tm, BRANCH_W), rmap),
        pl.BlockSpec((1, N_HEADS, tm), tmap), pl.BlockSpec((1, N_HEADS, tm), tmap),
        pl.BlockSpec((1, tm, LANES), rmap),
        pl.BlockSpec((1, 1, 8, BRANCH_W), lambda i, j: (i, j, 0, 0)),
    )
    in_specs = [
        pl.BlockSpec((1, tm, d), rmap),
        pl.BlockSpec((1, 1, d), lambda i, j: (i, 0, 0)),
        pl.BlockSpec((1, 1, d), lambda i, j: (i, 0, 0)),
        pl.BlockSpec(wp["w_t"].shape, const),
        pl.BlockSpec(wp["wft"].shape, const),
        pl.BlockSpec(wp["bf_col"].shape, const),
    ]
    return pl.pallas_call(
        functools.partial(_proj_body, tm=tm, tk=tk),
        grid=(b, nt),
        in_specs=in_specs,
        out_specs=out_specs,
        out_shape=out_shape,
        scratch_shapes=[pltpu.VMEM((N_HEADS, LANES), F32)],
        compiler_params=_cparams("arbitrary", "arbitrary"),
        name="proj_prompt",
    )(x, scale, shift, wp["w_t"], wp["wft"], wp["bf_col"])


def _masked_q(q_ref, hh, tq):
    pair = hh // 2
    qp = q_ref[0, pair * LANES:(pair + 1) * LANES, :]
    z = jnp.zeros((D_HEAD, tq), BF16)
    return jnp.concatenate([qp[:D_HEAD], z], axis=0) if hh % 2 == 0 else jnp.concatenate([z, qp[D_HEAD:]], axis=0)


def _flash_heads(n_heads, lhs_tile, rhs_ref, v_tile, past_bias, qi, diag_mask, ot_ref, m_ref, acc_ref, lg_refs):
    ahead = 2

    def score(hh, kv, mask):
        s = _dot(lhs_tile(hh // 2, kv), rhs_ref[hh])
        if mask is not None:
            return jnp.where(mask, s, NEG_INF)
        if past_bias is not None:
            return past_bias(hh, kv) + s
        return s

    def consume(hh, kv, slot):
        lg = lg_refs[hh][slot]
        m_old = m_ref[hh]
        m_new = jnp.maximum(m_old, jnp.max(lg, axis=0, keepdims=True))
        p = jnp.exp2(lg - m_new).astype(BF16)
        acc_ref[hh] = jnp.exp2(m_old - m_new) * acc_ref[hh] + _dot(v_tile(hh, kv), p)
        m_ref[hh] = m_new

    for hh in range(n_heads):
        m_ref[hh] = jnp.full(m_ref.shape[1:], NEG_INF, F32)
        acc_ref[hh] = jnp.zeros(acc_ref.shape[1:], F32)
        lg_refs[hh][0] = score(hh, qi, diag_mask)

    def body(kv, cur):
        slot = kv & 1
        pending = {hh: score(hh, kv, None) for hh in range(min(ahead, n_heads))}
        for hh in range(n_heads):
            consume(hh, cur, slot)
            lg_refs[hh][1 - slot] = pending.pop(hh)
            if hh + ahead < n_heads:
                pending[hh + ahead] = score(hh + ahead, kv, None)
        return kv

    cur = lax.fori_loop(0, qi, body, qi)
    for hh in range(n_heads):
        consume(hh, cur, qi & 1)
        acc = acc_ref[hh]
        ot_ref[hh * D_HEAD:(hh + 1) * D_HEAD, :] = acc[:D_HEAD] / acc[D_HEAD:D_HEAD + 1]


def _attention_scratch(hg, tq, tk):
    return [pltpu.VMEM((hg, 2 * LANES, tq), BF16), pltpu.VMEM((hg * D_HEAD, tq), F32),
            pltpu.VMEM((hg, 1, tq), F32), pltpu.VMEM((hg, V_ROWS, tq), F32)] + [
                pltpu.VMEM((2, tk, tq), F32) for _ in range(hg)]


def _fox_body(q_ref, k_ref, ka_ref, v_ref, c_ref, o_ref, rhs_ref, ot_ref, m_ref, acc_ref, *lg_refs, tq, tk, hg):
    qi = pl.program_id(2)
    rowi = lax.broadcasted_iota(jnp.int32, (LANES, tq), 0)
    causal = (lax.broadcasted_iota(jnp.int32, (tk, tq), 0) <= lax.broadcasted_iota(jnp.int32, (tk, tq), 1))
    for hh in range(hg):
        h = pl.program_id(1) * hg + hh
        hi, mid, lo = _split3(c_ref[0, pl.ds(h, 1), :] * LOG2E)
        aug = jnp.where((rowi < 3 * N_HEADS) & ((rowi & (N_HEADS - 1)) == h), 1.0, 0.0)
        aug = jnp.where(rowi == 3 * N_HEADS, hi, aug)
        aug = jnp.where(rowi == 3 * N_HEADS + 1, mid, aug)
        aug = jnp.where(rowi == 3 * N_HEADS + 2, lo, aug)
        rhs_ref[hh, 0:LANES, :] = _masked_q(q_ref, hh, tq)
        rhs_ref[hh, LANES:2 * LANES, :] = aug.astype(BF16)

    def lhs_tile(pair, kv):
        off = pl.multiple_of(kv * tk, tk)
        return jnp.concatenate([k_ref[0, pl.ds(off, tk), pair * LANES:(pair + 1) * LANES],
                                ka_ref[0, pl.ds(off, tk), :]], axis=1)

    def v_tile(hh, kv):
        return v_ref[0, kv, hh * V_ROWS:(hh + 1) * V_ROWS, :]

    _flash_heads(hg, lhs_tile, rhs_ref, v_tile, None, qi, causal, ot_ref, m_ref, acc_ref, lg_refs)
    o_ref[0] = ot_ref[...].T.astype(o_ref.dtype)


def _fox_prompt(qt, kr, kaug, vg, ct, tq, tk, hg):
    b, _, t = qt.shape
    nk = t // tk
    return pl.pallas_call(
        functools.partial(_fox_body, tq=tq, tk=tk, hg=hg),
        grid=(b, N_HEADS // hg, t // tq),
        in_specs=[
            pl.BlockSpec((1, hg * D_HEAD, tq), lambda i, g, j: (i, g, j)),
            pl.BlockSpec((1, t, hg * D_HEAD), lambda i, g, j: (i, 0, g)),
            pl.BlockSpec((1, t, LANES), lambda i, g, j: (i, 0, 0)),
            pl.BlockSpec((1, nk, hg * V_ROWS, tk), lambda i, g, j: (i, 0, g, 0)),
            pl.BlockSpec((1, N_HEADS, tq), lambda i, g, j: (i, 0, j)),
        ],
        out_specs=pl.BlockSpec((1, tq, hg * D_HEAD), lambda i, g, j: (i, j, g)),
        out_shape=jax.ShapeDtypeStruct((b, t, BRANCH_W), BF16),
        scratch_shapes=_attention_scratch(hg, tq, tk),
        compiler_params=_cparams("arbitrary", "arbitrary", "arbitrary"),
        name="fox_prompt",
    )(qt, kr, kaug, vg, ct)


def _moba_body(sl_ref, q_ref, k_ref, v_ref, km_ref, o_ref, sel_ref, pa_ref, rhs_ref, ot_ref, m_ref, acc_ref, *lg_refs,
               tq, tk, nb, hg):
    qi = pl.program_id(2)

    @pl.when((pl.program_id(0) == 0) & (pl.program_id(1) == 0) & (qi == 0))
    def _():
        shape = pa_ref.shape
        pos = lax.broadcasted_iota(jnp.int32, shape, 0).astype(F32)
        col = lax.broadcasted_iota(jnp.int32, shape, 1)
        p_hi = pos.astype(BF16).astype(F32)
        pa_ref[...] = jnp.where(col < 3, p_hi, jnp.where(col < 6, pos - p_hi, jnp.where(col < 9, 1.0, 0.0))).astype(BF16)

    rowi = lax.broadcasted_iota(jnp.int32, (LANES, tq), 0)
    causal = (lax.broadcasted_iota(jnp.int32, (tk, tq), 0) <= lax.broadcasted_iota(jnp.int32, (tk, tq), 1))
    blk = lax.broadcasted_iota(jnp.int32, (nb, tq), 0)
    tpos = (qi * tq + lax.broadcasted_iota(jnp.int32, (1, tq), 1)).astype(F32)
    for hh in range(hg):
        h = pl.program_id(1) * hg + hh
        qm = _masked_q(q_ref, hh, tq)

        pair = hh // 2
        km_hi, km_lo = _split2(km_ref[0][:, pair * LANES:(pair + 1) * LANES])
        sc = _dot(km_hi, qm) + _dot(km_lo, qm)
        rank = jnp.zeros((nb, tq), F32)
        for mm in range(nb):
            row = sc[mm:mm + 1, :]
            live = (mm < qi).astype(F32)
            rank = rank + jnp.where(blk > mm, jnp.where(row >= sc, live, 0.0), jnp.where(row > sc, live, 0.0))
        chosen = (blk < qi) & (rank < MOBA_TOPK)
        sel_ref[hh] = jnp.where(chosen, 0.0, NEG_INF)

        slope = jnp.full((1, tq), sl_ref[h], F32) * LOG2E
        s_hi, s_mid, s_lo = _split3(slope)
        a_hi, a_mid, a_lo = _split3(-(slope * tpos))
        aug = jnp.where((rowi == 0) | (rowi == 3), s_hi, 0.0)
        aug = jnp.where((rowi == 1) | (rowi == 4), s_mid, aug)
        aug = jnp.where((rowi == 2) | (rowi == 5), s_lo, aug)
        aug = jnp.where(rowi == 6, a_hi, aug)
        aug = jnp.where(rowi == 7, a_mid, aug)
        aug = jnp.where(rowi == 8, a_lo, aug)
        rhs_ref[hh, 0:LANES, :] = qm
        rhs_ref[hh, LANES:2 * LANES, :] = aug.astype(BF16)

    def lhs_tile(pair, kv):
        off = pl.multiple_of(kv * tk, tk)
        return jnp.concatenate([k_ref[0, pl.ds(off, tk), pair * LANES:(pair + 1) * LANES],
                                pa_ref[pl.ds(off, tk), :]], axis=1)

    def v_tile(hh, kv):
        return v_ref[0, kv, hh * V_ROWS:(hh + 1) * V_ROWS, :]

    def past_bias(hh, kv):
        return sel_ref[hh, pl.ds(kv, 1), :]

    _flash_heads(hg, lhs_tile, rhs_ref, v_tile, past_bias, qi, causal, ot_ref, m_ref, acc_ref, lg_refs)
    o_ref[0] = ot_ref[...].T.astype(o_ref.dtype)


def _moba_prompt(slopes, qt, kr, vg, kmean, tq, tk, hg):
    b, _, t = qt.shape
    nk = t // tk
    nb = kmean.shape[1]
    return pl.pallas_call(
        functools.partial(_moba_body, tq=tq, tk=tk, nb=nb, hg=hg),
        grid=(b, N_HEADS // hg, t // tq),
        in_specs=[
            pl.BlockSpec(memory_space=pltpu.SMEM),
            pl.BlockSpec((1, hg * D_HEAD, tq), lambda i, g, j: (i, g, j)),
            pl.BlockSpec((1, t, hg * D_HEAD), lambda i, g, j: (i, 0, g)),
            pl.BlockSpec((1, nk, hg * V_ROWS, tk), lambda i, g, j: (i, 0, g, 0)),
            pl.BlockSpec((1, nb, hg * D_HEAD), lambda i, g, j: (i, 0, g)),
        ],
        out_specs=pl.BlockSpec((1, tq, hg * D_HEAD), lambda i, g, j: (i, j, g)),
        scratch_shapes=[pltpu.VMEM((hg, nb, tq), F32), pltpu.VMEM((t, LANES), BF16)] + _attention_scratch(hg, tq, tk),
        out_shape=jax.ShapeDtypeStruct((b, t, BRANCH_W), BF16),
        compiler_params=_cparams("arbitrary", "arbitrary", "arbitrary"),
        name="moba_prompt",
    )(slopes, qt, kr, vg, kmean)


def _combine_body(x_ref, sc_ref, sh_ref, g_ref, oa_ref, ob_ref, wz_ref, wg_ref, wpa_ref, wpb_ref, wo_ref,
                  lng_ref, lnb_ref, y_ref, *, alpha):
    x = x_ref[0]
    hb = (x * (1.0 + sc_ref[0]) + sh_ref[0]).astype(BF16)
    z = _dot(hb, wz_ref[...])
    g = _dot(hb, wg_ref[...])
    za, zb = z[:, :BRANCH_W], z[:, BRANCH_W:]
    d = x.shape[-1]
    ua = (oa_ref[0].astype(F32) * (za * _sigmoid(za))).astype(BF16)
    ub = (ob_ref[0].astype(F32) * (zb * _sigmoid(zb))).astype(BF16)
    ya = _dot(ua, wpa_ref[...])
    yb = _dot(ub, wpb_ref[...])
    mix = (_sigmoid(g[:, :d]) * ya + _sigmoid(g[:, d:]) * yb).astype(BF16)
    s = _dot(mix, wo_ref[...])
    r = alpha * x + g_ref[0] * s
    mu = jnp.mean(r, axis=-1, keepdims=True)
    dev = r - mu
    var = jnp.mean(dev * dev, axis=-1, keepdims=True)
    y_ref[0] = dev * lax.rsqrt(var + LN_EPS) * lng_ref[...] + lnb_ref[...]


def _combine(x, scale, shift, gate, oa, ob, wp, alpha, tm):
    b, t, d = x.shape
    tmod = scale.shape[1]
    rmap = lambda i, j: (i, j, 0)
    mod_spec = pl.BlockSpec((1, 1, d), lambda i, j: (i, 0, 0)) if tmod == 1 else pl.BlockSpec((1, tm, d), rmap)
    const = lambda i, j: (0, 0)
    names = ("w_z", "w_g", "w_pa", "w_pb", "w_o", "ln_g", "ln_b")
    return pl.pallas_call(
        functools.partial(_combine_body, alpha=alpha),
        grid=(b, t // tm),
        in_specs=[pl.BlockSpec((1, tm, d), rmap), mod_spec, mod_spec, mod_spec,
                  pl.BlockSpec((1, tm, BRANCH_W), rmap), pl.BlockSpec((1, tm, BRANCH_W), rmap)]
                 + [pl.BlockSpec(wp[k].shape, const) for k in names],
        out_specs=pl.BlockSpec((1, tm, d), rmap),
        out_shape=jax.ShapeDtypeStruct((b, t, d), F32),
        compiler_params=_cparams("arbitrary", "arbitrary"),
        name="combine",
    )(x, scale, shift, gate, oa, ob, *[wp[k] for k in names])


def _proj_sample_body(x_ref, sc_ref, sh_ref, wr_ref, wt_ref, wft_ref, wfr_ref, bf_ref, bfr_ref,
                      qa_ref, ka_ref, va_ref, qb_ref, kb_ref, vb_ref, lfr_ref,
                      kat_ref, vat_ref, kbt_ref, vbt_ref, lft_ref):
    h32 = x_ref[...] * (1.0 + sc_ref[...]) + sh_ref[...]
    hb = h32.astype(BF16)
    hl = (h32 - hb.astype(F32)).astype(BF16)
    r = _dot(hb, wr_ref[...])
    scale = D_HEAD ** -0.5
    w = BRANCH_W
    qa_ref[...] = r[:, 0:w] * scale
    ka_ref[...] = r[:, w:2 * w]
    va_ref[...] = r[:, 2 * w:3 * w]
    qb_ref[...] = r[:, 3 * w:4 * w] * scale
    kb_ref[...] = r[:, 4 * w:5 * w]
    vb_ref[...] = r[:, 5 * w:6 * w]
    wr_hi, wr_lo = _split2(wfr_ref[...])
    flr = _dot(hb, wr_hi) + _dot(hl, wr_hi) + _dot(hb, wr_lo)
    lfr_ref[...] = _log_sigmoid(flr + bfr_ref[...])
    for g, ref in ((1, kat_ref), (2, vat_ref), (4, kbt_ref), (5, vbt_ref)):
        ref[...] = _dot_nt(wt_ref[g * w:(g + 1) * w, :], hb)
    lft_ref[...] = _log_sigmoid(_forget_logits_t(wft_ref, hb, hl) + bf_ref[:, 0:1])


def _project_sample(x, scale, shift, wp):
    n, d = x.shape
    row = jax.ShapeDtypeStruct((n, BRANCH_W), F32)
    col = jax.ShapeDtypeStruct((BRANCH_W, n), F32)
    out_shape = (row,) * 6 + (jax.ShapeDtypeStruct((n, LANES), F32),) + (col,) * 4 + (
        jax.ShapeDtypeStruct((N_HEADS, n), F32),)
    args = (x, scale, shift, wp["w_rows"], wp["w_t"], wp["wft"], wp["wfr"], wp["bf_col"], wp["bf_row"])
    return pl.pallas_call(
        _proj_sample_body,
        out_shape=out_shape,
        compiler_params=pltpu.CompilerParams(vmem_limit_bytes=VMEM_LIMIT),
        name="proj_sample",
    )(*args)


def _shift_right(n):
    return n.bit_length() - 1


def _sample_body(pt_ref, qa_ref, qb_ref, *refs, past_len, nstep, nbs, ppb, dec_seq, page):
    pps = nbs * ppb
    caches = refs[:5 * pps]
    mk, mv, fk, fv, fl = (caches[i * pps:(i + 1) * pps] for i in range(5))
    kan_ref, van_ref, kbn_ref, vbn_ref, lfn_ref, slope_ref, oa_ref, ob_ref = refs[5 * pps:5 * pps + 8]
    qda_ref, qdb_ref, ssc_ref, sm_ref, sl_ref, so_ref, fm_ref, fls_ref, facc_ref, fcar_ref = refs[5 * pps + 8:]
    del pt_ref
    b = pl.program_id(0)
    n = pl.program_id(1)
    nrow = dec_seq * N_HEADS
    nkey = pps * page
    ntok = kan_ref.shape[1]
    qsh = _shift_right(N_HEADS)

    def rows_iota(shape):
        return lax.broadcasted_iota(jnp.int32, shape, 0)

    def lanes_iota(shape):
        return lax.broadcasted_iota(jnp.int32, shape, 1)

    head_mask = (lanes_iota((nrow, BRANCH_W)) >> _shift_right(D_HEAD)) == (rows_iota((nrow, BRANCH_W)) & (N_HEADS - 1))

    def block_diag_q(q_ref):
        q = q_ref[0]
        q32 = jnp.concatenate([jnp.broadcast_to(q[i:i + 1, :], (N_HEADS, BRANCH_W)) for i in range(dec_seq)], axis=0)
        q32 = jnp.where(head_mask, q32, 0.0)
        hi = q32.astype(BF16)
        lo = (q32 - hi.astype(F32)).astype(BF16)
        return jnp.concatenate([hi, lo], axis=0)

    @pl.when(n == 0)
    def _():
        qda_ref[...] = block_diag_q(qa_ref)
        qdb_ref[...] = block_diag_q(qb_ref)
        ssc_ref[...] = jnp.full(ssc_ref.shape, NEG_INF, F32)
        sm_ref[...] = jnp.full(sm_ref.shape, NEG_INF, F32)
        sl_ref[...] = jnp.zeros(sl_ref.shape, F32)
        so_ref[...] = jnp.zeros(so_ref.shape, F32)
        fm_ref[...] = jnp.full(fm_ref.shape, NEG_INF, F32)
        fls_ref[...] = jnp.zeros(fls_ref.shape, F32)
        facc_ref[...] = jnp.zeros(facc_ref.shape, F32)
        fcar_ref[...] = jnp.zeros(fcar_ref.shape, F32)

    def qk(qd_ref, kt):
        s2 = _dot(qd_ref[...], kt)
        return s2[:nrow] + s2[nrow:]

    def bcast(col):
        return jnp.broadcast_to(col, (nrow, LANES))

    def softmax_part(lg):
        m = jnp.max(lg, axis=1, keepdims=True)
        p = jnp.exp(lg - m).astype(BF16)
        return m, p, jnp.sum(p.astype(F32), axis=1, keepdims=True)

    slope = slope_ref[:, 0:1]
    qrow = rows_iota((nrow, MOBA_BLOCK)) >> qsh

    def block_t(refs, j):
        return jnp.concatenate([r[0] for r in refs[j * ppb:(j + 1) * ppb]], axis=1).astype(BF16)

    s_moba = [qk(qda_ref, block_t(mk, j)) for j in range(nbs)]
    s_fox = [qk(qdb_ref, block_t(fk, j)) for j in range(nbs)]

    parts = []
    for j in range(nbs):
        score = jnp.sum(s_moba[j], axis=1, keepdims=True)
        dist = (past_len - (n * nbs + j) * MOBA_BLOCK + qrow - lanes_iota((nrow, MOBA_BLOCK))).astype(F32)
        m_n, p, l_n = softmax_part(s_moba[j] - slope * dist)
        parts.append((score, m_n, l_n, p))

    lf = jnp.concatenate([r[0] for r in fl], axis=1)
    c_blk = _lane_cumsum(lf) + fcar_ref[:, 0:1]
    fcar_ref[...] = jnp.broadcast_to(c_blk[:, nkey - 1:nkey], fcar_ref.shape)
    c_rows = jnp.concatenate([c_blk] * dec_seq, axis=0)
    u = jnp.concatenate(s_fox, axis=1) - c_rows
    m_old = fm_ref[:, 0:1]
    m_new = jnp.maximum(m_old, jnp.max(u, axis=1, keepdims=True))
    alpha = jnp.exp(m_old - m_new)
    p_fox = jnp.exp(u - m_new).astype(BF16)
    fls_ref[...] = bcast(alpha * fls_ref[:, 0:1] + jnp.sum(p_fox.astype(F32), axis=1, keepdims=True))
    fm_ref[...] = bcast(m_new)

    o_moba = [_dot_nt(parts[j][3], block_t(mv, j)) for j in range(nbs)]
    o_fox = _dot_nt(p_fox[:, 0:MOBA_BLOCK], block_t(fv, 0))
    for j in range(1, nbs):
        o_fox = o_fox + _dot_nt(p_fox[:, j * MOBA_BLOCK:(j + 1) * MOBA_BLOCK], block_t(fv, j))
    facc_ref[...] = alpha * facc_ref[...] + o_fox

    state = (ssc_ref, sm_ref, sl_ref, so_ref)
    slots = [[ref[i] for ref in state] for i in range(MOBA_TOPK)]
    for j in range(nbs):
        score, m_n, l_n, _ = parts[j]
        new = (bcast(score), bcast(m_n), bcast(l_n), o_moba[j])
        gt = [score > slots[i][0][:, 0:1] for i in range(MOBA_TOPK)]
        nxt = []
        for i in range(MOBA_TOPK):
            row = []
            for f in range(len(state)):
                keep = jnp.where(gt[i], new[f], slots[i][f])
                row.append(keep if i == 0 else jnp.where(gt[i - 1], slots[i - 1][f], keep))
            nxt.append(row)
        slots = nxt
    for i in range(MOBA_TOPK):
        for f, ref in enumerate(state):
            ref[i] = slots[i][f]

    @pl.when(n == nstep - 1)
    def _():
        lane = lanes_iota((nrow, ntok))
        qr = rows_iota((nrow, ntok)) >> qsh
        tok_q = lane & (dec_seq - 1)
        visible = ((lane >> _shift_right(dec_seq)) == b) & (tok_q <= qr)

        def collapse(o):
            return jnp.sum(jnp.where(head_mask, o, 0.0).reshape(dec_seq, N_HEADS, BRANCH_W), axis=1)

        s_own = qk(qda_ref, kan_ref[...].astype(BF16))
        lg = jnp.where(visible, s_own - slope * (qr - tok_q).astype(F32), NEG_INF)
        m_o, p_o, l_o = softmax_part(lg)
        o_o = _dot_nt(p_o, van_ref[...].astype(BF16))
        ms = [sm_ref[i][:, 0:1] for i in range(MOBA_TOPK)]
        m_tot = m_o
        for mi in ms:
            m_tot = jnp.maximum(m_tot, mi)
        w_o = jnp.exp(m_o - m_tot)
        num = w_o * o_o
        den = w_o * l_o
        for i in range(MOBA_TOPK):
            w_i = jnp.exp(ms[i] - m_tot)
            num = num + w_i * so_ref[i]
            den = den + w_i * sl_ref[i][:, 0:1]
        oa_ref[0] = collapse(num / den)

        c_new = _lane_cumsum(lfn_ref[...], seg=dec_seq)
        c_new_rows = jnp.concatenate([c_new] * dec_seq, axis=0)
        cq = jnp.sum(jnp.where(lane == b * dec_seq + qr, c_new_rows, 0.0), axis=1, keepdims=True)
        c_past = jnp.concatenate([fcar_ref[:, 0:1]] * dec_seq, axis=0)
        m_p = fm_ref[:, 0:1] + c_past + cq
        s_new = qk(qdb_ref, kbn_ref[...].astype(BF16))
        lg = jnp.where(visible, s_new + cq - c_new_rows, NEG_INF)
        m_o, p_o, l_o = softmax_part(lg)
        o_o = _dot_nt(p_o, vbn_ref[...].astype(BF16))
        m_tot = jnp.maximum(m_p, m_o)
        w_p = jnp.exp(m_p - m_tot)
        w_o = jnp.exp(m_o - m_tot)
        num = w_p * facc_ref[...] + w_o * o_o
        den = w_p * fls_ref[:, 0:1] + w_o * l_o
        ob_ref[0] = collapse(num / den)


def _sample_attention(page_table, qa, qb, caches, new_t, slope_rows, past_len, page):
    nb_, dec_seq, _ = qa.shape
    ppb = MOBA_BLOCK // page
    nblk = past_len // MOBA_BLOCK
    nbs = next(c for c in (SAMPLE_BLOCKS_PER_STEP, 2, 1) if nblk % c == 0)
    pps = nbs * ppb
    nrow = dec_seq * N_HEADS

    def page_specs(rows):
        return [pl.BlockSpec((1, rows, page), functools.partial(lambda i, n, pt, j: (pt[i, n * pps + j], 0, 0), j=j))
                for j in range(pps)]

    q_spec = pl.BlockSpec((1, dec_seq, BRANCH_W), lambda i, n, pt: (i, 0, 0))
    full = lambda a: pl.BlockSpec(a.shape, lambda i, n, pt: (0, 0))
    in_specs = [q_spec, q_spec]
    args = [qa, qb]
    for c in caches:
        in_specs += page_specs(c.shape[1])
        args += [c] * pps
    in_specs += [full(a) for a in new_t] + [full(slope_rows)]
    args += list(new_t) + [slope_rows]
    vm = lambda *s: pltpu.VMEM(s, F32)
    return pl.pallas_call(
        functools.partial(_sample_body, past_len=past_len, nstep=nblk // nbs, nbs=nbs, ppb=ppb, dec_seq=dec_seq,
                          page=page),
        grid_spec=pltpu.PrefetchScalarGridSpec(
            num_scalar_prefetch=1,
            grid=(nb_, nblk // nbs),
            in_specs=in_specs,
            out_specs=(q_spec, q_spec),
            scratch_shapes=[
                pltpu.VMEM((2 * nrow, BRANCH_W), BF16), pltpu.VMEM((2 * nrow, BRANCH_W), BF16),
                vm(MOBA_TOPK, nrow, LANES), vm(MOBA_TOPK, nrow, LANES), vm(MOBA_TOPK, nrow, LANES),
                vm(MOBA_TOPK, nrow, BRANCH_W),
                vm(nrow, LANES), vm(nrow, LANES), vm(nrow, BRANCH_W), vm(N_HEADS, LANES)],
        ),
        out_shape=(jax.ShapeDtypeStruct(qa.shape, F32), jax.ShapeDtypeStruct(qa.shape, F32)),
        compiler_params=_cparams("arbitrary", "arbitrary"),
        name="sample_attn",
    )(page_table, *args)


def _prep_weights(w_in, b_f, w_pa, w_pb, w_o, ln_g, ln_b):
    d = w_in.shape[0]
    w = BRANCH_W
    cols = {}
    off = 0
    for name, size in (("qa", w), ("ka", w), ("va", w), ("za", w), ("qb", w), ("kb", w), ("vb", w), ("zb", w),
                       ("f", N_HEADS), ("ga", d), ("gb", d)):
        cols[name] = w_in[:, off:off + size]
        off += size
    w_rows = jnp.concatenate([cols[k] for k in ("qa", "ka", "va", "qb", "kb", "vb")], axis=1).astype(BF16)
    wf = cols["f"]
    return {
        "w_rows": w_rows,
        "w_t": w_rows.T,
        "wft": jnp.pad(wf.T, ((0, N_HEADS), (0, 0))),
        "wfr": jnp.pad(wf, ((0, 0), (0, LANES - N_HEADS))),
        "bf_col": jnp.broadcast_to(b_f[:, None], (N_HEADS, LANES)),
        "bf_row": jnp.pad(b_f[None, :], ((0, 0), (0, LANES - N_HEADS))),
        "w_z": jnp.concatenate([cols["za"], cols["zb"]], axis=1).astype(BF16),
        "w_g": jnp.concatenate([cols["ga"], cols["gb"]], axis=1).astype(BF16),
        "w_pa": w_pa.astype(BF16), "w_pb": w_pb.astype(BF16), "w_o": w_o.astype(BF16),
        "ln_g": ln_g[None, :], "ln_b": ln_b[None, :],
    }


def _alibi_slopes():
    return 2.0 ** (-8.0 * (jnp.arange(N_HEADS, dtype=F32) + 1.0) / N_HEADS)


def _heads_last(a_t, b, t):
    return a_t.reshape(b, N_HEADS, D_HEAD, t).transpose(0, 3, 1, 2)


def kernel(x_prompt, x_sample, cache_moba_k, cache_moba_v, cache_fox_k, cache_fox_v, cache_fox_logf, page_table,
           c_prompt, c_sample, w_ada, b_ada, w_in, b_f, w_pa, w_pb, w_o, ln_g, ln_b):
    depth = w_ada.shape[0]
    bp, t, d = x_prompt.shape
    bs, dec_seq, _ = x_sample.shape
    n_pool, page = cache_moba_k.shape[1], cache_moba_k.shape[2]
    past_len = page_table.shape[1] * page
    assert t % MOBA_BLOCK == 0 and past_len % MOBA_BLOCK == 0 and MOBA_BLOCK % page == 0
    assert dec_seq & (dec_seq - 1) == 0 and (bs * dec_seq) % 8 == 0
    alpha = (2.0 * depth) ** 0.25
    tq = tk = MOBA_BLOCK
    tm = 512 if t % 512 == 0 else MOBA_BLOCK

    slopes = _alibi_slopes()
    slope_rows = jnp.broadcast_to(jnp.tile(slopes, dec_seq)[:, None], (dec_seq * N_HEADS, LANES))

    x_p = x_prompt
    x_s = x_sample.reshape(1, bs * dec_seq, d)
    outs = [[] for _ in range(10)]
    for l in range(depth):
        wp = _prep_weights(w_in[l], b_f[l], w_pa[l], w_pb[l], w_o[l], ln_g[l], ln_b[l])
        mod = _ada(jnp.concatenate([c_prompt, c_sample], axis=0), w_ada[l], b_ada[l])
        shift, scale, gate = (mod[:, i * d:(i + 1) * d] for i in range(3))
        mod_p = [m[:bp, None, :] for m in (scale, shift, gate)]
        mod_s = [jnp.repeat(m[bp:], dec_seq, axis=0)[None] for m in (scale, shift, gate)]

        (kat, vat, kbt, vbt, qat, qbt, vag, vbg, kar, kbr, lft, ct, fkaug, kmean) = _project_prompt(
            x_p, mod_p[0], mod_p[1], wp, tm, tk)
        nblk_tile = tm // MOBA_BLOCK
        kmean = kmean[:, :, :nblk_tile, :].reshape(bp, t // MOBA_BLOCK, BRANCH_W)
        oa = _moba_prompt(slopes, qat, kar, vag, kmean, tq, tk, HEADS_PER_STEP)
        ob = _fox_prompt(qbt, kbr, fkaug, vbg, ct, tq, tk, HEADS_PER_STEP)
        x_p = _combine(x_p, mod_p[0], mod_p[1], mod_p[2], oa, ob, wp, alpha, tm)
        for i, a in enumerate((kat, vat, kbt, vbt)):
            outs[i].append(_heads_last(a, bp, t))
        outs[4].append(lft.transpose(0, 2, 1))

        n_s = bs * dec_seq
        (qa_s, ka_s, va_s, qb_s, kb_s, vb_s, lfr_s, kat_s, vat_s, kbt_s, vbt_s, lft_s) = _project_sample(
            x_s[0], mod_s[0][0], mod_s[1][0], wp)
        feat_major = lambda c: c[l].transpose(0, 2, 3, 1).reshape(n_pool, BRANCH_W, page)
        caches = [feat_major(c) for c in (cache_moba_k, cache_moba_v, cache_fox_k, cache_fox_v)]
        caches.append(cache_fox_logf[l].transpose(0, 2, 1))
        oa_s, ob_s = _sample_attention(page_table, qa_s.reshape(bs, dec_seq, BRANCH_W),
                                       qb_s.reshape(bs, dec_seq, BRANCH_W), caches,
                                       (kat_s, vat_s, kbt_s, vbt_s, lft_s), slope_rows, past_len, page)
        x_s = _combine(x_s, mod_s[0], mod_s[1], mod_s[2], oa_s.reshape(1, n_s, BRANCH_W),
                       ob_s.reshape(1, n_s, BRANCH_W), wp, alpha, n_s)
        for i, a in enumerate((ka_s, va_s, kb_s, vb_s)):
            outs[5 + i].append(a.reshape(bs, dec_seq, N_HEADS, D_HEAD))
        outs[9].append(lfr_s[:, :N_HEADS].reshape(bs, dec_seq, N_HEADS))

    return (x_p, x_s.reshape(bs, dec_seq, d)) + tuple(jnp.stack(o) for o in outs)
```

```python
import functools

import jax
import jax.numpy as jnp
from jax import lax
from jax.experimental import pallas as pl
from jax.experimental.pallas import tpu as pltpu

F32 = jnp.float32
BF16 = jnp.bfloat16

N_HEADS = 8
D_HEAD = 64
BRANCH_W = N_HEADS * D_HEAD
MOBA_BLOCK = 256
MOBA_TOPK = 3
LN_EPS = 1e-5
LANES = 128
V_ROWS = 80
VMEM_LIMIT = 56 * 1024 * 1024
NEG_INF = float("-inf")
SKIP_MARGIN = 140.0
LOG2E = 1.4426950408889634
SAMPLE_BLOCKS_PER_STEP = 8
HEADS_PER_STEP = 8


def _cparams(*sem):
    return pltpu.CompilerParams(dimension_semantics=sem, vmem_limit_bytes=VMEM_LIMIT)


def _split3(x):
    hi = x.astype(BF16).astype(F32)
    r = x - hi
    mid = r.astype(BF16).astype(F32)
    lo = (r - mid).astype(BF16).astype(F32)
    return hi, mid, lo


def _lane_cumsum(x, seg=None):
    n = x.shape[-1]
    lane = lax.broadcasted_iota(jnp.int32, x.shape, x.ndim - 1)
    pos = lane if seg is None else lane & (seg - 1)
    limit = n if seg is None else seg
    s = 1
    while s < limit:
        x = x + jnp.where(pos >= s, pltpu.roll(x, s, axis=x.ndim - 1), 0.0)
        s *= 2
    return x


def _log_sigmoid(x):
    return jnp.minimum(x, 0.0) - jnp.log(1.0 + jnp.exp(-jnp.abs(x)))


def _sigmoid(x):
    return 1.0 / (1.0 + jnp.exp(-x))


def _dot(a, b):
    return jnp.dot(a, b, preferred_element_type=F32)


def _dot_nt(a, b):
    return lax.dot_general(a, b, (((1,), (1,)), ((), ())), preferred_element_type=F32)


def _ada_body(c_ref, w_ref, b_ref, o_ref):
    o_ref[...] = lax.dot_general(c_ref[...], w_ref[...], (((1,), (0,)), ((), ())),
                                 precision=lax.Precision.HIGHEST,
                                 preferred_element_type=F32) + b_ref[...]


def _ada(c_all, w_ada, b_ada):
    n, d = c_all.shape
    d3 = w_ada.shape[1]
    tn = 1024
    return pl.pallas_call(
        _ada_body,
        grid=(d3 // tn,),
        in_specs=[pl.BlockSpec((n, d), lambda j: (0, 0)),
                  pl.BlockSpec((d, tn), lambda j: (0, j)),
                  pl.BlockSpec((1, tn), lambda j: (0, j))],
        out_specs=pl.BlockSpec((n, tn), lambda j: (0, j)),
        out_shape=jax.ShapeDtypeStruct((n, d3), F32),
        compiler_params=_cparams("arbitrary"),
        name="ada",
    )(c_all, w_ada, b_ada.reshape(1, d3))


def _split2(x):
    hi = x.astype(BF16)
    return hi, (x - hi.astype(F32)).astype(BF16)


def _forget_logits_t(wft_ref, hb, hl):
    w_hi, w_lo = _split2(wft_ref[...])
    return (_dot_nt(w_hi, hb) + _dot_nt(w_hi, hl) + _dot_nt(w_lo, hb))[:N_HEADS]


def _proj_body(x_ref, sc_ref, sh_ref, wt_ref, wft_ref, bf_ref,
               kat_ref, vat_ref, kbt_ref, vbt_ref, qat_ref, qbt_ref, vag_ref, vbg_ref,
               kar_ref, kbr_ref, lft_ref, ct_ref, fkaug_ref, kmean_ref, qn_ref, kn_ref, carry_ref, *, tm, tk):
    t = pl.program_id(1)

    @pl.when(t == 0)
    def _():
        carry_ref[...] = jnp.zeros_like(carry_ref)

    h32 = x_ref[0] * (1.0 + sc_ref[0]) + sh_ref[0]
    hb = h32.astype(BF16)
    hl = (h32 - hb.astype(F32)).astype(BF16)

    ones_rows = jnp.where(lax.broadcasted_iota(jnp.int32, (V_ROWS - D_HEAD, tk), 0) == 0, 1.0, 0.0).astype(BF16)

    r = _dot_nt(wt_ref[...], hb)

    def group(g):
        return r[g * BRANCH_W:(g + 1) * BRANCH_W, :]

    def store_v(r, vt_ref, vg_ref):
        vt_ref[0] = r
        rb = r.astype(BF16)
        for j in range(tm // tk):
            for h in range(N_HEADS):
                vg_ref[0, j, h * V_ROWS:h * V_ROWS + D_HEAD, :] = rb[h * D_HEAD:(h + 1) * D_HEAD, j * tk:(j + 1) * tk]
                vg_ref[0, j, h * V_ROWS + D_HEAD:(h + 1) * V_ROWS, :] = ones_rows

    scale = D_HEAD ** -0.5 * LOG2E
    qat_ref[0] = (group(0) * scale).astype(BF16)
    ka_t = group(1)
    kat_ref[0] = ka_t
    ka_r = ka_t.T
    kar_ref[0] = ka_r.astype(BF16)
    nblk = tm // MOBA_BLOCK
    kmean_ref[0, 0] = jnp.zeros(kmean_ref.shape[2:], F32)
    kmean_ref[0, 0, 0:nblk, :] = jnp.mean(ka_r.reshape(nblk, MOBA_BLOCK, BRANCH_W), axis=1)
    store_v(group(2), vat_ref, vag_ref)
    qb_t = group(3) * scale
    qbt_ref[0] = qb_t.astype(BF16)
    kb_t = group(4)
    kbt_ref[0] = kb_t
    kbr_ref[0] = kb_t.T.astype(BF16)

    def tile_norms(a_t):
        n2 = jnp.sum((a_t * a_t).reshape(N_HEADS, D_HEAD, tm), axis=1)
        lane = lax.broadcasted_iota(jnp.int32, (N_HEADS, LANES), 1)
        out = jnp.zeros((N_HEADS, LANES), F32)
        for j in range(tm // tk):
            out = jnp.where(lane == j, jnp.sqrt(jnp.max(n2[:, j * tk:(j + 1) * tk], axis=1, keepdims=True)), out)
        return out

    qn_ref[0, 0] = tile_norms(qb_t)
    kn_ref[0, 0] = tile_norms(kb_t)
    store_v(group(5), vbt_ref, vbg_ref)

    lf = _log_sigmoid(_forget_logits_t(wft_ref, hb, hl) + bf_ref[:, 0:1])
    lft_ref[0] = lf
    c = _lane_cumsum(lf) + carry_ref[:, 0:1]
    ct_ref[0] = c
    carry_ref[...] = jnp.broadcast_to(c[:, tm - 1:tm], carry_ref.shape)

    hi, mid, lo = _split3(-c * LOG2E)
    ones_blk = jnp.where(lax.broadcasted_iota(jnp.int32, (N_HEADS, tm), 0) < 3, 1.0, 0.0)
    aug_t = jnp.concatenate([hi, mid, lo, ones_blk, jnp.zeros((LANES - 4 * N_HEADS, tm), F32)], axis=0)
    fkaug_ref[0] = aug_t.T.astype(BF16)


def _project_prompt(x, scale, shift, wp, tm, tk):
    b, t, d = x.shape
    nt = t // tm
    f32o = lambda rows: jax.ShapeDtypeStruct((b, rows, t), F32)
    out_shape = (
        f32o(BRANCH_W), f32o(BRANCH_W), f32o(BRANCH_W), f32o(BRANCH_W),
        jax.ShapeDtypeStruct((b, BRANCH_W, t), BF16), jax.ShapeDtypeStruct((b, BRANCH_W, t), BF16),
        jax.ShapeDtypeStruct((b, t // tk, N_HEADS * V_ROWS, tk), BF16),
        jax.ShapeDtypeStruct((b, t // tk, N_HEADS * V_ROWS, tk), BF16),
        jax.ShapeDtypeStruct((b, t, BRANCH_W), BF16), jax.ShapeDtypeStruct((b, t, BRANCH_W), BF16),
        f32o(N_HEADS), f32o(N_HEADS),
        jax.ShapeDtypeStruct((b, t, LANES), BF16),
        jax.ShapeDtypeStruct((b, nt, 8, BRANCH_W), F32),
        jax.ShapeDtypeStruct((b, nt, N_HEADS, LANES), F32),
        jax.ShapeDtypeStruct((b, nt, N_HEADS, LANES), F32),
    )
    tmap = lambda i, j: (i, 0, j)
    rmap = lambda i, j: (i, j, 0)
    vg_spec = pl.BlockSpec((1, tm // tk, N_HEADS * V_ROWS, tk), lambda i, j: (i, j, 0, 0))
    const = lambda i, j: (0, 0)
    out_specs = (
        pl.BlockSpec((1, BRANCH_W, tm), tmap), pl.BlockSpec((1, BRANCH_W, tm), tmap),
        pl.BlockSpec((1, BRANCH_W, tm), tmap), pl.BlockSpec((1, BRANCH_W, tm), tmap),
        pl.BlockSpec((1, BRANCH_W, tm), tmap), pl.BlockSpec((1, BRANCH_W, tm), tmap),
        vg_spec, vg_spec,
        pl.BlockSpec((1, tm, BRANCH_W), rmap), pl.BlockSpec((1, tm, BRANCH_W), rmap),
        pl.BlockSpec((1, N_HEADS, tm), tmap), pl.BlockSpec((1, N_HEADS, tm), tmap),
        pl.BlockSpec((1, tm, LANES), rmap),
        pl.BlockSpec((1, 1, 8, BRANCH_W), lambda i, j: (i, j, 0, 0)),
        pl.BlockSpec((1, 1, N_HEADS, LANES), lambda i, j: (i, j, 0, 0)),
        pl.BlockSpec((1, 1, N_HEADS, LANES), lambda i, j: (i, j, 0, 0)),
    )
    in_specs = [
        pl.BlockSpec((1, tm, d), rmap),
        pl.BlockSpec((1, 1, d), lambda i, j: (i, 0, 0)),
        pl.BlockSpec((1, 1, d), lambda i, j: (i, 0, 0)),
        pl.BlockSpec(wp["w_t"].shape, const),
        pl.BlockSpec(wp["wft"].shape, const),
        pl.BlockSpec(wp["bf_col"].shape, const),
    ]
    return pl.pallas_call(
        functools.partial(_proj_body, tm=tm, tk=tk),
        grid=(b, nt),
        in_specs=in_specs,
        out_specs=out_specs,
        out_shape=out_shape,
        scratch_shapes=[pltpu.VMEM((N_HEADS, LANES), F32)],
        compiler_params=_cparams("arbitrary", "arbitrary"),
        name="proj_prompt",
    )(x, scale, shift, wp["w_t"], wp["wft"], wp["bf_col"])


def _masked_q(q_ref, hh, tq):
    pair = hh // 2
    qp = q_ref[0, pair * LANES:(pair + 1) * LANES, :]
    z = jnp.zeros((D_HEAD, tq), BF16)
    return jnp.concatenate([qp[:D_HEAD], z], axis=0) if hh % 2 == 0 else jnp.concatenate([z, qp[D_HEAD:]], axis=0)


def _flash_heads(n_heads, lhs_tile, rhs_ref, v_tile, past_bias, qi, diag_mask, ot_ref, m_ref, acc_ref, lg_refs, start=0):
    ahead = 2

    def score(hh, kv, mask):
        s = _dot(lhs_tile(hh // 2, kv), rhs_ref[hh])
        if mask is not None:
            return jnp.where(mask, s, NEG_INF)
        if past_bias is not None:
            return past_bias(hh, kv) + s
        return s

    def consume(hh, kv, slot):
        lg = lg_refs[hh][slot]
        m_old = m_ref[hh]
        m_new = jnp.maximum(m_old, jnp.max(lg, axis=0, keepdims=True))
        p = jnp.exp2(lg - m_new).astype(BF16)
        acc_ref[hh] = jnp.exp2(m_old - m_new) * acc_ref[hh] + _dot(v_tile(hh, kv), p)
        m_ref[hh] = m_new

    for hh in range(n_heads):
        m_ref[hh] = jnp.full(m_ref.shape[1:], NEG_INF, F32)
        acc_ref[hh] = jnp.zeros(acc_ref.shape[1:], F32)
        lg_refs[hh][start & 1] = score(hh, qi, diag_mask)

    def body(kv, cur):
        slot = kv & 1
        pending = {hh: score(hh, kv, None) for hh in range(min(ahead, n_heads))}
        for hh in range(n_heads):
            consume(hh, cur, slot)
            lg_refs[hh][1 - slot] = pending.pop(hh)
            if hh + ahead < n_heads:
                pending[hh + ahead] = score(hh + ahead, kv, None)
        return kv

    cur = lax.fori_loop(start, qi, body, qi)
    for hh in range(n_heads):
        consume(hh, cur, qi & 1)
        acc = acc_ref[hh]
        ot_ref[hh * D_HEAD:(hh + 1) * D_HEAD, :] = acc[:D_HEAD] / acc[D_HEAD:D_HEAD + 1]


def _attention_scratch(hg, tq, tk):
    return [pltpu.VMEM((hg, 2 * LANES, tq), BF16), pltpu.VMEM((hg * D_HEAD, tq), F32),
            pltpu.VMEM((hg, 1, tq), F32), pltpu.VMEM((hg, V_ROWS, tq), F32)] + [
                pltpu.VMEM((2, tk, tq), F32) for _ in range(hg)]


def _first_live_tile(qn_ref, kn_ref, c0_ref, c1_ref, h0, n_heads, qi):
    kn = kn_ref[0]
    tile = lax.broadcasted_iota(jnp.int32, kn.shape, 1)
    head = lax.broadcasted_iota(jnp.int32, kn.shape, 0)
    at_qi = lambda a: jnp.sum(jnp.where(tile == qi, a, 0.0), axis=1, keepdims=True)
    ub = (at_qi(qn_ref[0]) * 1.02) * (kn + at_qi(kn)) + (at_qi(c0_ref[0]) - c1_ref[0]) * LOG2E + 0.1
    live = (ub > -SKIP_MARGIN) & (tile < qi) & (head >= h0) & (head < h0 + n_heads)
    return jnp.min(jnp.where(live, tile, qi))


def _fox_body(qn_ref, kn_ref, c0_ref, c1_ref, q_ref, k_ref, ka_ref, v_ref, c_ref, o_ref,
              rhs_ref, ot_ref, m_ref, acc_ref, *lg_refs, tq, tk, hg):
    qi = pl.program_id(2)
    start = _first_live_tile(qn_ref, kn_ref, c0_ref, c1_ref, pl.program_id(1) * hg, hg, qi)
    rowi = lax.broadcasted_iota(jnp.int32, (LANES, tq), 0)
    causal = (lax.broadcasted_iota(jnp.int32, (tk, tq), 0) <= lax.broadcasted_iota(jnp.int32, (tk, tq), 1))
    for hh in range(hg):
        h = pl.program_id(1) * hg + hh
        hi, mid, lo = _split3(c_ref[0, pl.ds(h, 1), :] * LOG2E)
        aug = jnp.where((rowi < 3 * N_HEADS) & ((rowi & (N_HEADS - 1)) == h), 1.0, 0.0)
        aug = jnp.where(rowi == 3 * N_HEADS, hi, aug)
        aug = jnp.where(rowi == 3 * N_HEADS + 1, mid, aug)
        aug = jnp.where(rowi == 3 * N_HEADS + 2, lo, aug)
        rhs_ref[hh, 0:LANES, :] = _masked_q(q_ref, hh, tq)
        rhs_ref[hh, LANES:2 * LANES, :] = aug.astype(BF16)

    def lhs_tile(pair, kv):
        off = pl.multiple_of(kv * tk, tk)
        return jnp.concatenate([k_ref[0, pl.ds(off, tk), pair * LANES:(pair + 1) * LANES],
                                ka_ref[0, pl.ds(off, tk), :]], axis=1)

    def v_tile(hh, kv):
        return v_ref[0, kv, hh * V_ROWS:(hh + 1) * V_ROWS, :]

    _flash_heads(hg, lhs_tile, rhs_ref, v_tile, None, qi, causal, ot_ref, m_ref, acc_ref, lg_refs, start)
    o_ref[0] = ot_ref[...].T.astype(o_ref.dtype)


def _fox_prompt(qt, kr, kaug, vg, ct, qn, kn, tq, tk, hg):
    b, _, t = qt.shape
    nk = t // tk
    c_first = ct[:, :, 0::tk]
    c_last = ct[:, :, tk - 1::tk]
    table = pl.BlockSpec((1, N_HEADS, nk), lambda i, g, j: (i, 0, 0))
    return pl.pallas_call(
        functools.partial(_fox_body, tq=tq, tk=tk, hg=hg),
        grid=(b, N_HEADS // hg, t // tq),
        in_specs=[
            table, table, table, table,
            pl.BlockSpec((1, hg * D_HEAD, tq), lambda i, g, j: (i, g, j)),
            pl.BlockSpec((1, t, hg * D_HEAD), lambda i, g, j: (i, 0, g)),
            pl.BlockSpec((1, t, LANES), lambda i, g, j: (i, 0, 0)),
            pl.BlockSpec((1, nk, hg * V_ROWS, tk), lambda i, g, j: (i, 0, g, 0)),
            pl.BlockSpec((1, N_HEADS, tq), lambda i, g, j: (i, 0, j)),
        ],
        out_specs=pl.BlockSpec((1, tq, hg * D_HEAD), lambda i, g, j: (i, j, g)),
        out_shape=jax.ShapeDtypeStruct((b, t, BRANCH_W), BF16),
        scratch_shapes=_attention_scratch(hg, tq, tk),
        compiler_params=_cparams("arbitrary", "arbitrary", "arbitrary"),
        name="fox_prompt",
    )(qn, kn, c_first, c_last, qt, kr, kaug, vg, ct)


def _moba_body(sl_ref, q_ref, k_ref, v_ref, km_ref, o_ref, sel_ref, pa_ref, rhs_ref, ot_ref, m_ref, acc_ref, *lg_refs,
               tq, tk, nb, hg):
    qi = pl.program_id(2)

    @pl.when((pl.program_id(0) == 0) & (pl.program_id(1) == 0) & (qi == 0))
    def _():
        shape = pa_ref.shape
        pos = lax.broadcasted_iota(jnp.int32, shape, 0).astype(F32)
        col = lax.broadcasted_iota(jnp.int32, shape, 1)
        p_hi = pos.astype(BF16).astype(F32)
        pa_ref[...] = jnp.where(col < 3, p_hi, jnp.where(col < 6, pos - p_hi, jnp.where(col < 9, 1.0, 0.0))).astype(BF16)

    rowi = lax.broadcasted_iota(jnp.int32, (LANES, tq), 0)
    causal = (lax.broadcasted_iota(jnp.int32, (tk, tq), 0) <= lax.broadcasted_iota(jnp.int32, (tk, tq), 1))
    blk = lax.broadcasted_iota(jnp.int32, (nb, tq), 0)
    tpos = (qi * tq + lax.broadcasted_iota(jnp.int32, (1, tq), 1)).astype(F32)
    for hh in range(hg):
        h = pl.program_id(1) * hg + hh
        qm = _masked_q(q_ref, hh, tq)

        pair = hh // 2
        km_hi, km_lo = _split2(km_ref[0][:, pair * LANES:(pair + 1) * LANES])
        sc = _dot(km_hi, qm) + _dot(km_lo, qm)
        rank = jnp.zeros((nb, tq), F32)
        for mm in range(nb):
            row = sc[mm:mm + 1, :]
            live = (mm < qi).astype(F32)
            rank = rank + jnp.where(blk > mm, jnp.where(row >= sc, live, 0.0), jnp.where(row > sc, live, 0.0))
        chosen = (blk < qi) & (rank < MOBA_TOPK)
        sel_ref[hh] = jnp.where(chosen, 0.0, NEG_INF)

        slope = jnp.full((1, tq), sl_ref[h], F32) * LOG2E
        s_hi, s_mid, s_lo = _split3(slope)
        a_hi, a_mid, a_lo = _split3(-(slope * tpos))
        aug = jnp.where((rowi == 0) | (rowi == 3), s_hi, 0.0)
        aug = jnp.where((rowi == 1) | (rowi == 4), s_mid, aug)
        aug = jnp.where((rowi == 2) | (rowi == 5), s_lo, aug)
        aug = jnp.where(rowi == 6, a_hi, aug)
        aug = jnp.where(rowi == 7, a_mid, aug)
        aug = jnp.where(rowi == 8, a_lo, aug)
        rhs_ref[hh, 0:LANES, :] = qm
        rhs_ref[hh, LANES:2 * LANES, :] = aug.astype(BF16)

    def lhs_tile(pair, kv):
        off = pl.multiple_of(kv * tk, tk)
        return jnp.concatenate([k_ref[0, pl.ds(off, tk), pair * LANES:(pair + 1) * LANES],
                                pa_ref[pl.ds(off, tk), :]], axis=1)

    def v_tile(hh, kv):
        return v_ref[0, kv, hh * V_ROWS:(hh + 1) * V_ROWS, :]

    def past_bias(hh, kv):
        return sel_ref[hh, pl.ds(kv, 1), :]

    _flash_heads(hg, lhs_tile, rhs_ref, v_tile, past_bias, qi, causal, ot_ref, m_ref, acc_ref, lg_refs)
    o_ref[0] = ot_ref[...].T.astype(o_ref.dtype)


def _moba_prompt(slopes, qt, kr, vg, kmean, tq, tk, hg):
    b, _, t = qt.shape
    nk = t // tk
    nb = kmean.shape[1]
    return pl.pallas_call(
        functools.partial(_moba_body, tq=tq, tk=tk, nb=nb, hg=hg),
        grid=(b, N_HEADS // hg, t // tq),
        in_specs=[
            pl.BlockSpec(memory_space=pltpu.SMEM),
            pl.BlockSpec((1, hg * D_HEAD, tq), lambda i, g, j: (i, g, j)),
            pl.BlockSpec((1, t, hg * D_HEAD), lambda i, g, j: (i, 0, g)),
            pl.BlockSpec((1, nk, hg * V_ROWS, tk), lambda i, g, j: (i, 0, g, 0)),
            pl.BlockSpec((1, nb, hg * D_HEAD), lambda i, g, j: (i, 0, g)),
        ],
        out_specs=pl.BlockSpec((1, tq, hg * D_HEAD), lambda i, g, j: (i, j, g)),
        scratch_shapes=[pltpu.VMEM((hg, nb, tq), F32), pltpu.VMEM((t, LANES), BF16)] + _attention_scratch(hg, tq, tk),
        out_shape=jax.ShapeDtypeStruct((b, t, BRANCH_W), BF16),
        compiler_params=_cparams("arbitrary", "arbitrary", "arbitrary"),
        name="moba_prompt",
    )(slopes, qt, kr, vg, kmean)


def _combine_body(x_ref, sc_ref, sh_ref, g_ref, oa_ref, ob_ref, wz_ref, wg_ref, wpa_ref, wpb_ref, wo_ref,
                  lng_ref, lnb_ref, y_ref, *, alpha):
    x = x_ref[0]
    hb = (x * (1.0 + sc_ref[0]) + sh_ref[0]).astype(BF16)
    z = _dot(hb, wz_ref[...])
    g = _dot(hb, wg_ref[...])
    za, zb = z[:, :BRANCH_W], z[:, BRANCH_W:]
    d = x.shape[-1]
    ua = (oa_ref[0].astype(F32) * (za * _sigmoid(za))).astype(BF16)
    ub = (ob_ref[0].astype(F32) * (zb * _sigmoid(zb))).astype(BF16)
    ya = _dot(ua, wpa_ref[...])
    yb = _dot(ub, wpb_ref[...])
    mix = (_sigmoid(g[:, :d]) * ya + _sigmoid(g[:, d:]) * yb).astype(BF16)
    s = _dot(mix, wo_ref[...])
    r = alpha * x + g_ref[0] * s
    mu = jnp.mean(r, axis=-1, keepdims=True)
    dev = r - mu
    var = jnp.mean(dev * dev, axis=-1, keepdims=True)
    y_ref[0] = dev * lax.rsqrt(var + LN_EPS) * lng_ref[...] + lnb_ref[...]


def _combine(x, scale, shift, gate, oa, ob, wp, alpha, tm):
    b, t, d = x.shape
    tmod = scale.shape[1]
    rmap = lambda i, j: (i, j, 0)
    mod_spec = pl.BlockSpec((1, 1, d), lambda i, j: (i, 0, 0)) if tmod == 1 else pl.BlockSpec((1, tm, d), rmap)
    const = lambda i, j: (0, 0)
    names = ("w_z", "w_g", "w_pa", "w_pb", "w_o", "ln_g", "ln_b")
    return pl.pallas_call(
        functools.partial(_combine_body, alpha=alpha),
        grid=(b, t // tm),
        in_specs=[pl.BlockSpec((1, tm, d), rmap), mod_spec, mod_spec, mod_spec,
                  pl.BlockSpec((1, tm, BRANCH_W), rmap), pl.BlockSpec((1, tm, BRANCH_W), rmap)]
                 + [pl.BlockSpec(wp[k].shape, const) for k in names],
        out_specs=pl.BlockSpec((1, tm, d), rmap),
        out_shape=jax.ShapeDtypeStruct((b, t, d), F32),
        compiler_params=_cparams("arbitrary", "arbitrary"),
        name="combine",
    )(x, scale, shift, gate, oa, ob, *[wp[k] for k in names])


def _proj_sample_body(x_ref, sc_ref, sh_ref, wr_ref, wt_ref, wft_ref, wfr_ref, bf_ref, bfr_ref,
                      qa_ref, ka_ref, va_ref, qb_ref, kb_ref, vb_ref, lfr_ref,
                      kat_ref, vat_ref, kbt_ref, vbt_ref, lft_ref):
    h32 = x_ref[...] * (1.0 + sc_ref[...]) + sh_ref[...]
    hb = h32.astype(BF16)
    hl = (h32 - hb.astype(F32)).astype(BF16)
    r = _dot(hb, wr_ref[...])
    scale = D_HEAD ** -0.5
    w = BRANCH_W
    qa_ref[...] = r[:, 0:w] * scale
    ka_ref[...] = r[:, w:2 * w]
    va_ref[...] = r[:, 2 * w:3 * w]
    qb_ref[...] = r[:, 3 * w:4 * w] * scale
    kb_ref[...] = r[:, 4 * w:5 * w]
    vb_ref[...] = r[:, 5 * w:6 * w]
    wr_hi, wr_lo = _split2(wfr_ref[...])
    flr = _dot(hb, wr_hi) + _dot(hl, wr_hi) + _dot(hb, wr_lo)
    lfr_ref[...] = _log_sigmoid(flr + bfr_ref[...])
    for g, ref in ((1, kat_ref), (2, vat_ref), (4, kbt_ref), (5, vbt_ref)):
        ref[...] = _dot_nt(wt_ref[g * w:(g + 1) * w, :], hb)
    lft_ref[...] = _log_sigmoid(_forget_logits_t(wft_ref, hb, hl) + bf_ref[:, 0:1])


def _project_sample(x, scale, shift, wp):
    n, d = x.shape
    row = jax.ShapeDtypeStruct((n, BRANCH_W), F32)
    col = jax.ShapeDtypeStruct((BRANCH_W, n), F32)
    out_shape = (row,) * 6 + (jax.ShapeDtypeStruct((n, LANES), F32),) + (col,) * 4 + (
        jax.ShapeDtypeStruct((N_HEADS, n), F32),)
    args = (x, scale, shift, wp["w_rows"], wp["w_t"], wp["wft"], wp["wfr"], wp["bf_col"], wp["bf_row"])
    return pl.pallas_call(
        _proj_sample_body,
        out_shape=out_shape,
        compiler_params=pltpu.CompilerParams(vmem_limit_bytes=VMEM_LIMIT),
        name="proj_sample",
    )(*args)


def _shift_right(n):
    return n.bit_length() - 1


def _sample_body(pt_ref, qa_ref, qb_ref, *refs, past_len, nstep, nbs, ppb, dec_seq, page):
    pps = nbs * ppb
    caches = refs[:5 * pps]
    mk, mv, fk, fv, fl = (caches[i * pps:(i + 1) * pps] for i in range(5))
    kan_ref, van_ref, kbn_ref, vbn_ref, lfn_ref, slope_ref, oa_ref, ob_ref = refs[5 * pps:5 * pps + 8]
    qda_ref, qdb_ref, ssc_ref, sm_ref, sl_ref, so_ref, fm_ref, fls_ref, facc_ref, fcar_ref = refs[5 * pps + 8:]
    del pt_ref
    b = pl.program_id(0)
    n = pl.program_id(1)
    nrow = dec_seq * N_HEADS
    nkey = pps * page
    ntok = kan_ref.shape[1]
    qsh = _shift_right(N_HEADS)

    def rows_iota(shape):
        return lax.broadcasted_iota(jnp.int32, shape, 0)

    def lanes_iota(shape):
        return lax.broadcasted_iota(jnp.int32, shape, 1)

    head_mask = (lanes_iota((nrow, BRANCH_W)) >> _shift_right(D_HEAD)) == (rows_iota((nrow, BRANCH_W)) & (N_HEADS - 1))

    def block_diag_q(q_ref):
        q = q_ref[0]
        q32 = jnp.concatenate([jnp.broadcast_to(q[i:i + 1, :], (N_HEADS, BRANCH_W)) for i in range(dec_seq)], axis=0)
        q32 = jnp.where(head_mask, q32, 0.0)
        hi = q32.astype(BF16)
        lo = (q32 - hi.astype(F32)).astype(BF16)
        return jnp.concatenate([hi, lo], axis=0)

    @pl.when(n == 0)
    def _():
        qda_ref[...] = block_diag_q(qa_ref)
        qdb_ref[...] = block_diag_q(qb_ref)
        ssc_ref[...] = jnp.full(ssc_ref.shape, NEG_INF, F32)
        sm_ref[...] = jnp.full(sm_ref.shape, NEG_INF, F32)
        sl_ref[...] = jnp.zeros(sl_ref.shape, F32)
        so_ref[...] = jnp.zeros(so_ref.shape, F32)
        fm_ref[...] = jnp.full(fm_ref.shape, NEG_INF, F32)
        fls_ref[...] = jnp.zeros(fls_ref.shape, F32)
        facc_ref[...] = jnp.zeros(facc_ref.shape, F32)
        fcar_ref[...] = jnp.zeros(fcar_ref.shape, F32)

    def qk(qd_ref, kt):
        s2 = _dot(qd_ref[...], kt)
        return s2[:nrow] + s2[nrow:]

    def bcast(col):
        return jnp.broadcast_to(col, (nrow, LANES))

    def softmax_part(lg):
        m = jnp.max(lg, axis=1, keepdims=True)
        p = jnp.exp(lg - m).astype(BF16)
        return m, p, jnp.sum(p.astype(F32), axis=1, keepdims=True)

    slope = slope_ref[:, 0:1]
    qrow = rows_iota((nrow, MOBA_BLOCK)) >> qsh

    def block_t(refs, j):
        return jnp.concatenate([r[0] for r in refs[j * ppb:(j + 1) * ppb]], axis=1).astype(BF16)

    s_moba = [qk(qda_ref, block_t(mk, j)) for j in range(nbs)]
    s_fox = [qk(qdb_ref, block_t(fk, j)) for j in range(nbs)]

    parts = []
    for j in range(nbs):
        score = jnp.sum(s_moba[j], axis=1, keepdims=True)
        dist = (past_len - (n * nbs + j) * MOBA_BLOCK + qrow - lanes_iota((nrow, MOBA_BLOCK))).astype(F32)
        m_n, p, l_n = softmax_part(s_moba[j] - slope * dist)
        parts.append((score, m_n, l_n, p))

    lf = jnp.concatenate([r[0] for r in fl], axis=1)
    c_blk = _lane_cumsum(lf) + fcar_ref[:, 0:1]
    fcar_ref[...] = jnp.broadcast_to(c_blk[:, nkey - 1:nkey], fcar_ref.shape)
    c_rows = jnp.concatenate([c_blk] * dec_seq, axis=0)
    u = jnp.concatenate(s_fox, axis=1) - c_rows
    m_old = fm_ref[:, 0:1]
    m_new = jnp.maximum(m_old, jnp.max(u, axis=1, keepdims=True))
    alpha = jnp.exp(m_old - m_new)
    p_fox = jnp.exp(u - m_new).astype(BF16)
    fls_ref[...] = bcast(alpha * fls_ref[:, 0:1] + jnp.sum(p_fox.astype(F32), axis=1, keepdims=True))
    fm_ref[...] = bcast(m_new)

    o_moba = [_dot_nt(parts[j][3], block_t(mv, j)) for j in range(nbs)]
    o_fox = _dot_nt(p_fox[:, 0:MOBA_BLOCK], block_t(fv, 0))
    for j in range(1, nbs):
        o_fox = o_fox + _dot_nt(p_fox[:, j * MOBA_BLOCK:(j + 1) * MOBA_BLOCK], block_t(fv, j))
    facc_ref[...] = alpha * facc_ref[...] + o_fox

    state = (ssc_ref, sm_ref, sl_ref, so_ref)
    slots = [[ref[i] for ref in state] for i in range(MOBA_TOPK)]
    for j in range(nbs):
        score, m_n, l_n, _ = parts[j]
        new = (bcast(score), bcast(m_n), bcast(l_n), o_moba[j])
        gt = [score > slots[i][0][:, 0:1] for i in range(MOBA_TOPK)]
        nxt = []
        for i in range(MOBA_TOPK):
            row = []
            for f in range(len(state)):
                keep = jnp.where(gt[i], new[f], slots[i][f])
                row.append(keep if i == 0 else jnp.where(gt[i - 1], slots[i - 1][f], keep))
            nxt.append(row)
        slots = nxt
    for i in range(MOBA_TOPK):
        for f, ref in enumerate(state):
            ref[i] = slots[i][f]

    @pl.when(n == nstep - 1)
    def _():
        lane = lanes_iota((nrow, ntok))
        qr = rows_iota((nrow, ntok)) >> qsh
        tok_q = lane & (dec_seq - 1)
        visible = ((lane >> _shift_right(dec_seq)) == b) & (tok_q <= qr)

        def collapse(o):
            return jnp.sum(jnp.where(head_mask, o, 0.0).reshape(dec_seq, N_HEADS, BRANCH_W), axis=1)

        s_own = qk(qda_ref, kan_ref[...].astype(BF16))
        lg = jnp.where(visible, s_own - slope * (qr - tok_q).astype(F32), NEG_INF)
        m_o, p_o, l_o = softmax_part(lg)
        o_o = _dot_nt(p_o, van_ref[...].astype(BF16))
        ms = [sm_ref[i][:, 0:1] for i in range(MOBA_TOPK)]
        m_tot = m_o
        for mi in ms:
            m_tot = jnp.maximum(m_tot, mi)
        w_o = jnp.exp(m_o - m_tot)
        num = w_o * o_o
        den = w_o * l_o
        for i in range(MOBA_TOPK):
            w_i = jnp.exp(ms[i] - m_tot)
            num = num + w_i * so_ref[i]
            den = den + w_i * sl_ref[i][:, 0:1]
        oa_ref[0] = collapse(num / den)

        c_new = _lane_cumsum(lfn_ref[...], seg=dec_seq)
        c_new_rows = jnp.concatenate([c_new] * dec_seq, axis=0)
        cq = jnp.sum(jnp.where(lane == b * dec_seq + qr, c_new_rows, 0.0), axis=1, keepdims=True)
        c_past = jnp.concatenate([fcar_ref[:, 0:1]] * dec_seq, axis=0)
        m_p = fm_ref[:, 0:1] + c_past + cq
        s_new = qk(qdb_ref, kbn_ref[...].astype(BF16))
        lg = jnp.where(visible, s_new + cq - c_new_rows, NEG_INF)
        m_o, p_o, l_o = softmax_part(lg)
        o_o = _dot_nt(p_o, vbn_ref[...].astype(BF16))
        m_tot = jnp.maximum(m_p, m_o)
        w_p = jnp.exp(m_p - m_tot)
        w_o = jnp.exp(m_o - m_tot)
        num = w_p * facc_ref[...] + w_o * o_o
        den = w_p * fls_ref[:, 0:1] + w_o * l_o
        ob_ref[0] = collapse(num / den)


def _sample_attention(page_table, qa, qb, caches, new_t, slope_rows, past_len, page):
    nb_, dec_seq, _ = qa.shape
    ppb = MOBA_BLOCK // page
    nblk = past_len // MOBA_BLOCK
    nbs = next(c for c in (SAMPLE_BLOCKS_PER_STEP, 2, 1) if nblk % c == 0)
    pps = nbs * ppb
    nrow = dec_seq * N_HEADS

    def page_specs(rows):
        return [pl.BlockSpec((1, rows, page), functools.partial(lambda i, n, pt, j: (pt[i, n * pps + j], 0, 0), j=j))
                for j in range(pps)]

    q_spec = pl.BlockSpec((1, dec_seq, BRANCH_W), lambda i, n, pt: (i, 0, 0))
    full = lambda a: pl.BlockSpec(a.shape, lambda i, n, pt: (0, 0))
    in_specs = [q_spec, q_spec]
    args = [qa, qb]
    for c in caches:
        in_specs += page_specs(c.shape[1])
        args += [c] * pps
    in_specs += [full(a) for a in new_t] + [full(slope_rows)]
    args += list(new_t) + [slope_rows]
    vm = lambda *s: pltpu.VMEM(s, F32)
    return pl.pallas_call(
        functools.partial(_sample_body, past_len=past_len, nstep=nblk // nbs, nbs=nbs, ppb=ppb, dec_seq=dec_seq,
                          page=page),
        grid_spec=pltpu.PrefetchScalarGridSpec(
            num_scalar_prefetch=1,
            grid=(nb_, nblk // nbs),
            in_specs=in_specs,
            out_specs=(q_spec, q_spec),
            scratch_shapes=[
                pltpu.VMEM((2 * nrow, BRANCH_W), BF16), pltpu.VMEM((2 * nrow, BRANCH_W), BF16),
                vm(MOBA_TOPK, nrow, LANES), vm(MOBA_TOPK, nrow, LANES), vm(MOBA_TOPK, nrow, LANES),
                vm(MOBA_TOPK, nrow, BRANCH_W),
                vm(nrow, LANES), vm(nrow, LANES), vm(nrow, BRANCH_W), vm(N_HEADS, LANES)],
        ),
        out_shape=(jax.ShapeDtypeStruct(qa.shape, F32), jax.ShapeDtypeStruct(qa.shape, F32)),
        compiler_params=_cparams("arbitrary", "arbitrary"),
        name="sample_attn",
    )(page_table, *args)


def _prep_weights(w_in, b_f, w_pa, w_pb, w_o, ln_g, ln_b):
    d = w_in.shape[0]
    w = BRANCH_W
    cols = {}
    off = 0
    for name, size in (("qa", w), ("ka", w), ("va", w), ("za", w), ("qb", w), ("kb", w), ("vb", w), ("zb", w),
                       ("f", N_HEADS), ("ga", d), ("gb", d)):
        cols[name] = w_in[:, off:off + size]
        off += size
    w_rows = jnp.concatenate([cols[k] for k in ("qa", "ka", "va", "qb", "kb", "vb")], axis=1).astype(BF16)
    wf = cols["f"]
    return {
        "w_rows": w_rows,
        "w_t": w_rows.T,
        "wft": jnp.pad(wf.T, ((0, N_HEADS), (0, 0))),
        "wfr": jnp.pad(wf, ((0, 0), (0, LANES - N_HEADS))),
        "bf_col": jnp.broadcast_to(b_f[:, None], (N_HEADS, LANES)),
        "bf_row": jnp.pad(b_f[None, :], ((0, 0), (0, LANES - N_HEADS))),
        "w_z": jnp.concatenate([cols["za"], cols["zb"]], axis=1).astype(BF16),
        "w_g": jnp.concatenate([cols["ga"], cols["gb"]], axis=1).astype(BF16),
        "w_pa": w_pa.astype(BF16), "w_pb": w_pb.astype(BF16), "w_o": w_o.astype(BF16),
        "ln_g": ln_g[None, :], "ln_b": ln_b[None, :],
    }


def _alibi_slopes():
    return 2.0 ** (-8.0 * (jnp.arange(N_HEADS, dtype=F32) + 1.0) / N_HEADS)


def _heads_last(a_t, b, t):
    return a_t.reshape(b, N_HEADS, D_HEAD, t).transpose(0, 3, 1, 2)


def kernel(x_prompt, x_sample, cache_moba_k, cache_moba_v, cache_fox_k, cache_fox_v, cache_fox_logf, page_table,
           c_prompt, c_sample, w_ada, b_ada, w_in, b_f, w_pa, w_pb, w_o, ln_g, ln_b):
    depth = w_ada.shape[0]
    bp, t, d = x_prompt.shape
    bs, dec_seq, _ = x_sample.shape
    n_pool, page = cache_moba_k.shape[1], cache_moba_k.shape[2]
    past_len = page_table.shape[1] * page
    assert t % MOBA_BLOCK == 0 and past_len % MOBA_BLOCK == 0 and MOBA_BLOCK % page == 0
    assert dec_seq & (dec_seq - 1) == 0 and (bs * dec_seq) % 8 == 0
    alpha = (2.0 * depth) ** 0.25
    tq = tk = MOBA_BLOCK
    tm = 512 if t % 512 == 0 else MOBA_BLOCK

    slopes = _alibi_slopes()
    slope_rows = jnp.broadcast_to(jnp.tile(slopes, dec_seq)[:, None], (dec_seq * N_HEADS, LANES))

    x_p = x_prompt
    x_s = x_sample.reshape(1, bs * dec_seq, d)
    outs = [[] for _ in range(10)]
    for l in range(depth):
        wp = _prep_weights(w_in[l], b_f[l], w_pa[l], w_pb[l], w_o[l], ln_g[l], ln_b[l])
        mod = _ada(jnp.concatenate([c_prompt, c_sample], axis=0), w_ada[l], b_ada[l])
        shift, scale, gate = (mod[:, i * d:(i + 1) * d] for i in range(3))
        mod_p = [m[:bp, None, :] for m in (scale, shift, gate)]
        mod_s = [jnp.repeat(m[bp:], dec_seq, axis=0)[None] for m in (scale, shift, gate)]

        (kat, vat, kbt, vbt, qat, qbt, vag, vbg, kar, kbr, lft, ct, fkaug, kmean, qn, kn) = _project_prompt(
            x_p, mod_p[0], mod_p[1], wp, tm, tk)
        nblk_tile = tm // MOBA_BLOCK
        kmean = kmean[:, :, :nblk_tile, :].reshape(bp, t // MOBA_BLOCK, BRANCH_W)
        per_tile = lambda a: a[:, :, :, :tm // tk].transpose(0, 2, 1, 3).reshape(bp, N_HEADS, t // tk)
        oa = _moba_prompt(slopes, qat, kar, vag, kmean, tq, tk, HEADS_PER_STEP)
        ob = _fox_prompt(qbt, kbr, fkaug, vbg, ct, per_tile(qn), per_tile(kn), tq, tk, HEADS_PER_STEP)
        x_p = _combine(x_p, mod_p[0], mod_p[1], mod_p[2], oa, ob, wp, alpha, tm)
        for i, a in enumerate((kat, vat, kbt, vbt)):
            outs[i].append(_heads_last(a, bp, t))
        outs[4].append(lft.transpose(0, 2, 1))

        n_s = bs * dec_seq
        (qa_s, ka_s, va_s, qb_s, kb_s, vb_s, lfr_s, kat_s, vat_s, kbt_s, vbt_s, lft_s) = _project_sample(
            x_s[0], mod_s[0][0], mod_s[1][0], wp)
        feat_major = lambda c: c[l].transpose(0, 2, 3, 1).reshape(n_pool, BRANCH_W, page)
        caches = [feat_major(c) for c in (cache_moba_k, cache_moba_v, cache_fox_k, cache_fox_v)]
        caches.append(cache_fox_logf[l].transpose(0, 2, 1))
        oa_s, ob_s = _sample_attention(page_table, qa_s.reshape(bs, dec_seq, BRANCH_W),
                                       qb_s.reshape(bs, dec_seq, BRANCH_W), caches,
                                       (kat_s, vat_s, kbt_s, vbt_s, lft_s), slope_rows, past_len, page)
        x_s = _combine(x_s, mod_s[0], mod_s[1], mod_s[2], oa_s.reshape(1, n_s, BRANCH_W),
                       ob_s.reshape(1, n_s, BRANCH_W), wp, alpha, n_s)
        for i, a in enumerate((ka_s, va_s, kb_s, vb_s)):
            outs[5 + i].append(a.reshape(bs, dec_seq, N_HEADS, D_HEAD))
        outs[9].append(lfr_s[:, :N_HEADS].reshape(bs, dec_seq, N_HEADS))

    return (x_p, x_s.reshape(bs, dec_seq, d)) + tuple(jnp.stack(o) for o in outs)
```

```python
import functools

import jax
import jax.numpy as jnp
from jax import lax
from jax.experimental import pallas as pl
from jax.experimental.pallas import tpu as pltpu

F32 = jnp.float32
BF16 = jnp.bfloat16

N_HEADS = 8
D_HEAD = 64
BRANCH_W = N_HEADS * D_HEAD
MOBA_BLOCK = 256
MOBA_TOPK = 3
LN_EPS = 1e-5
LANES = 128
V_ROWS = 80
VMEM_LIMIT = 56 * 1024 * 1024
NEG_INF = float("-inf")
SKIP_MARGIN = 140.0
LOG2E = 1.4426950408889634
SAMPLE_BLOCKS_PER_STEP = 8
MOBA_LATE_HEADS = 2
HEADS_PER_STEP = 8


def _cparams(*sem):
    return pltpu.CompilerParams(dimension_semantics=sem, vmem_limit_bytes=VMEM_LIMIT)


def _split3(x):
    hi = x.astype(BF16).astype(F32)
    r = x - hi
    mid = r.astype(BF16).astype(F32)
    lo = (r - mid).astype(BF16).astype(F32)
    return hi, mid, lo


def _lane_cumsum(x, seg=None):
    n = x.shape[-1]
    lane = lax.broadcasted_iota(jnp.int32, x.shape, x.ndim - 1)
    pos = lane if seg is None else lane & (seg - 1)
    limit = n if seg is None else seg
    s = 1
    while s < limit:
        x = x + jnp.where(pos >= s, pltpu.roll(x, s, axis=x.ndim - 1), 0.0)
        s *= 2
    return x


def _log_sigmoid(x):
    return jnp.minimum(x, 0.0) - jnp.log(1.0 + jnp.exp(-jnp.abs(x)))


def _sigmoid(x):
    return 1.0 / (1.0 + jnp.exp(-x))


def _dot(a, b):
    return jnp.dot(a, b, preferred_element_type=F32)


def _dot_nt(a, b):
    return lax.dot_general(a, b, (((1,), (1,)), ((), ())), preferred_element_type=F32)


def _ada_body(c_ref, w_ref, b_ref, o_ref):
    o_ref[...] = lax.dot_general(c_ref[...], w_ref[...], (((1,), (0,)), ((), ())),
                                 precision=lax.Precision.HIGHEST,
                                 preferred_element_type=F32) + b_ref[...]


def _ada(c_all, w_ada, b_ada):
    n, d = c_all.shape
    d3 = w_ada.shape[1]
    tn = 1024
    return pl.pallas_call(
        _ada_body,
        grid=(d3 // tn,),
        in_specs=[pl.BlockSpec((n, d), lambda j: (0, 0)),
                  pl.BlockSpec((d, tn), lambda j: (0, j)),
                  pl.BlockSpec((1, tn), lambda j: (0, j))],
        out_specs=pl.BlockSpec((n, tn), lambda j: (0, j)),
        out_shape=jax.ShapeDtypeStruct((n, d3), F32),
        compiler_params=_cparams("arbitrary"),
        name="ada",
    )(c_all, w_ada, b_ada.reshape(1, d3))


def _split2(x):
    hi = x.astype(BF16)
    return hi, (x - hi.astype(F32)).astype(BF16)


def _forget_logits_t(wft_ref, hb, hl):
    w_hi, w_lo = _split2(wft_ref[...])
    return (_dot_nt(w_hi, hb) + _dot_nt(w_hi, hl) + _dot_nt(w_lo, hb))[:N_HEADS]


def _proj_body(x_ref, sc_ref, sh_ref, wt_ref, wft_ref, bf_ref,
               kat_ref, vat_ref, kbt_ref, vbt_ref, qat_ref, qbt_ref, vag_ref, vbg_ref,
               kar_ref, kbr_ref, lft_ref, ct_ref, fkaug_ref, kmean_ref, qn_ref, kn_ref, qna_ref, kna_ref,
               carry_ref, *, tm, tk):
    t = pl.program_id(1)

    @pl.when(t == 0)
    def _():
        carry_ref[...] = jnp.zeros_like(carry_ref)

    h32 = x_ref[0] * (1.0 + sc_ref[0]) + sh_ref[0]
    hb = h32.astype(BF16)
    hl = (h32 - hb.astype(F32)).astype(BF16)

    ones_rows = jnp.where(lax.broadcasted_iota(jnp.int32, (V_ROWS - D_HEAD, tk), 0) == 0, 1.0, 0.0).astype(BF16)

    r = _dot_nt(wt_ref[...], hb)

    def group(g):
        return r[g * BRANCH_W:(g + 1) * BRANCH_W, :]

    def store_v(r, vt_ref, vg_ref):
        vt_ref[0] = r
        rb = r.astype(BF16)
        for j in range(tm // tk):
            for h in range(N_HEADS):
                vg_ref[0, j, h * V_ROWS:h * V_ROWS + D_HEAD, :] = rb[h * D_HEAD:(h + 1) * D_HEAD, j * tk:(j + 1) * tk]
                vg_ref[0, j, h * V_ROWS + D_HEAD:(h + 1) * V_ROWS, :] = ones_rows

    scale = D_HEAD ** -0.5 * LOG2E
    qa_t = group(0) * scale
    qat_ref[0] = qa_t.astype(BF16)
    ka_t = group(1)
    kat_ref[0] = ka_t
    ka_r = ka_t.T
    kar_ref[0] = ka_r.astype(BF16)
    nblk = tm // MOBA_BLOCK
    kmean_ref[0, 0] = jnp.zeros(kmean_ref.shape[2:], F32)
    kmean_ref[0, 0, 0:nblk, :] = jnp.mean(ka_r.reshape(nblk, MOBA_BLOCK, BRANCH_W), axis=1)
    store_v(group(2), vat_ref, vag_ref)
    qb_t = group(3) * scale
    qbt_ref[0] = qb_t.astype(BF16)
    kb_t = group(4)
    kbt_ref[0] = kb_t
    kbr_ref[0] = kb_t.T.astype(BF16)

    def tile_norms(a_t):
        n2 = jnp.sum((a_t * a_t).reshape(N_HEADS, D_HEAD, tm), axis=1)
        lane = lax.broadcasted_iota(jnp.int32, (N_HEADS, LANES), 1)
        out = jnp.zeros((N_HEADS, LANES), F32)
        for j in range(tm // tk):
            out = jnp.where(lane == j, jnp.sqrt(jnp.max(n2[:, j * tk:(j + 1) * tk], axis=1, keepdims=True)), out)
        return out

    qn_ref[0, 0] = tile_norms(qb_t)
    kn_ref[0, 0] = tile_norms(kb_t)
    qna_ref[0, 0] = tile_norms(qa_t)
    kna_ref[0, 0] = tile_norms(ka_t)
    store_v(group(5), vbt_ref, vbg_ref)

    lf = _log_sigmoid(_forget_logits_t(wft_ref, hb, hl) + bf_ref[:, 0:1])
    lft_ref[0] = lf
    c = _lane_cumsum(lf) + carry_ref[:, 0:1]
    ct_ref[0] = c
    carry_ref[...] = jnp.broadcast_to(c[:, tm - 1:tm], carry_ref.shape)

    hi, mid, lo = _split3(-c * LOG2E)
    ones_blk = jnp.where(lax.broadcasted_iota(jnp.int32, (N_HEADS, tm), 0) < 3, 1.0, 0.0)
    aug_t = jnp.concatenate([hi, mid, lo, ones_blk, jnp.zeros((LANES - 4 * N_HEADS, tm), F32)], axis=0)
    fkaug_ref[0] = aug_t.T.astype(BF16)


def _project_prompt(x, scale, shift, wp, tm, tk):
    b, t, d = x.shape
    nt = t // tm
    f32o = lambda rows: jax.ShapeDtypeStruct((b, rows, t), F32)
    out_shape = (
        f32o(BRANCH_W), f32o(BRANCH_W), f32o(BRANCH_W), f32o(BRANCH_W),
        jax.ShapeDtypeStruct((b, BRANCH_W, t), BF16), jax.ShapeDtypeStruct((b, BRANCH_W, t), BF16),
        jax.ShapeDtypeStruct((b, t // tk, N_HEADS * V_ROWS, tk), BF16),
        jax.ShapeDtypeStruct((b, t // tk, N_HEADS * V_ROWS, tk), BF16),
        jax.ShapeDtypeStruct((b, t, BRANCH_W), BF16), jax.ShapeDtypeStruct((b, t, BRANCH_W), BF16),
        f32o(N_HEADS), f32o(N_HEADS),
        jax.ShapeDtypeStruct((b, t, LANES), BF16),
        jax.ShapeDtypeStruct((b, nt, 8, BRANCH_W), F32),
        jax.ShapeDtypeStruct((b, nt, N_HEADS, LANES), F32),
        jax.ShapeDtypeStruct((b, nt, N_HEADS, LANES), F32),
        jax.ShapeDtypeStruct((b, nt, N_HEADS, LANES), F32),
        jax.ShapeDtypeStruct((b, nt, N_HEADS, LANES), F32),
    )
    tmap = lambda i, j: (i, 0, j)
    rmap = lambda i, j: (i, j, 0)
    vg_spec = pl.BlockSpec((1, tm // tk, N_HEADS * V_ROWS, tk), lambda i, j: (i, j, 0, 0))
    const = lambda i, j: (0, 0)
    out_specs = (
        pl.BlockSpec((1, BRANCH_W, tm), tmap), pl.BlockSpec((1, BRANCH_W, tm), tmap),
        pl.BlockSpec((1, BRANCH_W, tm), tmap), pl.BlockSpec((1, BRANCH_W, tm), tmap),
        pl.BlockSpec((1, BRANCH_W, tm), tmap), pl.BlockSpec((1, BRANCH_W, tm), tmap),
        vg_spec, vg_spec,
        pl.BlockSpec((1, tm, BRANCH_W), rmap), pl.BlockSpec((1, tm, BRANCH_W), rmap),
        pl.BlockSpec((1, N_HEADS, tm), tmap), pl.BlockSpec((1, N_HEADS, tm), tmap),
        pl.BlockSpec((1, tm, LANES), rmap),
        pl.BlockSpec((1, 1, 8, BRANCH_W), lambda i, j: (i, j, 0, 0)),
        pl.BlockSpec((1, 1, N_HEADS, LANES), lambda i, j: (i, j, 0, 0)),
        pl.BlockSpec((1, 1, N_HEADS, LANES), lambda i, j: (i, j, 0, 0)),
        pl.BlockSpec((1, 1, N_HEADS, LANES), lambda i, j: (i, j, 0, 0)),
        pl.BlockSpec((1, 1, N_HEADS, LANES), lambda i, j: (i, j, 0, 0)),
    )
    in_specs = [
        pl.BlockSpec((1, tm, d), rmap),
        pl.BlockSpec((1, 1, d), lambda i, j: (i, 0, 0)),
        pl.BlockSpec((1, 1, d), lambda i, j: (i, 0, 0)),
        pl.BlockSpec(wp["w_t"].shape, const),
        pl.BlockSpec(wp["wft"].shape, const),
        pl.BlockSpec(wp["bf_col"].shape, const),
    ]
    return pl.pallas_call(
        functools.partial(_proj_body, tm=tm, tk=tk),
        grid=(b, nt),
        in_specs=in_specs,
        out_specs=out_specs,
        out_shape=out_shape,
        scratch_shapes=[pltpu.VMEM((N_HEADS, LANES), F32)],
        compiler_params=_cparams("arbitrary", "arbitrary"),
        name="proj_prompt",
    )(x, scale, shift, wp["w_t"], wp["wft"], wp["bf_col"])


def _masked_q(q_ref, hh, tq):
    pair = hh // 2
    qp = q_ref[0, pair * LANES:(pair + 1) * LANES, :]
    z = jnp.zeros((D_HEAD, tq), BF16)
    return jnp.concatenate([qp[:D_HEAD], z], axis=0) if hh % 2 == 0 else jnp.concatenate([z, qp[D_HEAD:]], axis=0)


def _flash_heads(n_heads, lhs_tile, rhs_ref, v_tile, past_bias, qi, diag_mask, ot_ref, m_ref, acc_ref, lg_refs,
                 start=0, n_late=0, start_late=None):
    ahead = 2
    if start_late is None:
        n_late, start_late = 0, start
    first = [start_late if hh < n_late else start for hh in range(n_heads)]

    def score(hh, kv, mask):
        s = _dot(lhs_tile(hh // 2, kv), rhs_ref[hh])
        if mask is not None:
            return jnp.where(mask, s, NEG_INF)
        if past_bias is not None:
            return past_bias(hh, kv) + s
        return s

    def consume(hh, kv, slot):
        lg = lg_refs[hh][slot]
        m_old = m_ref[hh]
        m_new = jnp.maximum(m_old, jnp.max(lg, axis=0, keepdims=True))
        p = jnp.exp2(lg - m_new).astype(BF16)
        acc_ref[hh] = jnp.exp2(m_old - m_new) * acc_ref[hh] + _dot(v_tile(hh, kv), p)
        m_ref[hh] = m_new

    for hh in range(n_heads):
        m_ref[hh] = jnp.full(m_ref.shape[1:], NEG_INF, F32)
        acc_ref[hh] = jnp.zeros(acc_ref.shape[1:], F32)
        lg_refs[hh][first[hh] & 1] = score(hh, qi, diag_mask)

    def sweep(heads, kv, cur_of):
        slot = kv & 1
        pending = {hh: score(hh, kv, None) for hh in heads[:ahead]}
        for i, hh in enumerate(heads):
            consume(hh, cur_of(hh), slot)
            lg_refs[hh][1 - slot] = pending.pop(hh)
            if i + ahead < len(heads):
                pending[heads[i + ahead]] = score(heads[i + ahead], kv, None)

    early = list(range(n_late, n_heads))
    every = list(range(n_heads))

    def body_early(kv, cur):
        sweep(early, kv, lambda hh: cur)
        return kv

    def body_all(kv, curs):
        sweep(every, kv, lambda hh: curs[0] if hh < n_late else curs[1])
        return kv, kv

    cur_early = lax.fori_loop(start, start_late, body_early, qi) if n_late else qi
    cur_late, cur_early = lax.fori_loop(start_late, qi, body_all, (qi, cur_early))
    for hh in range(n_heads):
        consume(hh, cur_late if hh < n_late else cur_early, qi & 1)
        acc = acc_ref[hh]
        ot_ref[hh * D_HEAD:(hh + 1) * D_HEAD, :] = acc[:D_HEAD] / acc[D_HEAD:D_HEAD + 1]


def _attention_scratch(hg, tq, tk):
    return [pltpu.VMEM((hg, 2 * LANES, tq), BF16), pltpu.VMEM((hg * D_HEAD, tq), F32),
            pltpu.VMEM((hg, 1, tq), F32), pltpu.VMEM((hg, V_ROWS, tq), F32)] + [
                pltpu.VMEM((2, tk, tq), F32) for _ in range(hg)]


def _first_live_tile(qn_ref, kn_ref, c0_ref, c1_ref, h0, n_heads, qi):
    kn = kn_ref[0]
    tile = lax.broadcasted_iota(jnp.int32, kn.shape, 1)
    head = lax.broadcasted_iota(jnp.int32, kn.shape, 0)
    at_qi = lambda a: jnp.sum(jnp.where(tile == qi, a, 0.0), axis=1, keepdims=True)
    ub = (at_qi(qn_ref[0]) * 1.02) * (kn + at_qi(kn)) + (at_qi(c0_ref[0]) - c1_ref[0]) * LOG2E + 0.1
    live = (ub > -SKIP_MARGIN) & (tile < qi) & (head >= h0) & (head < h0 + n_heads)
    return jnp.min(jnp.where(live, tile, qi))


def _fox_body(qn_ref, kn_ref, c0_ref, c1_ref, q_ref, k_ref, ka_ref, v_ref, c_ref, o_ref,
              rhs_ref, ot_ref, m_ref, acc_ref, *lg_refs, tq, tk, hg):
    qi = pl.program_id(2)
    start = _first_live_tile(qn_ref, kn_ref, c0_ref, c1_ref, pl.program_id(1) * hg, hg, qi)
    rowi = lax.broadcasted_iota(jnp.int32, (LANES, tq), 0)
    causal = (lax.broadcasted_iota(jnp.int32, (tk, tq), 0) <= lax.broadcasted_iota(jnp.int32, (tk, tq), 1))
    for hh in range(hg):
        h = pl.program_id(1) * hg + hh
        hi, mid, lo = _split3(c_ref[0, pl.ds(h, 1), :] * LOG2E)
        aug = jnp.where((rowi < 3 * N_HEADS) & ((rowi & (N_HEADS - 1)) == h), 1.0, 0.0)
        aug = jnp.where(rowi == 3 * N_HEADS, hi, aug)
        aug = jnp.where(rowi == 3 * N_HEADS + 1, mid, aug)
        aug = jnp.where(rowi == 3 * N_HEADS + 2, lo, aug)
        rhs_ref[hh, 0:LANES, :] = _masked_q(q_ref, hh, tq)
        rhs_ref[hh, LANES:2 * LANES, :] = aug.astype(BF16)

    def lhs_tile(pair, kv):
        off = pl.multiple_of(kv * tk, tk)
        return jnp.concatenate([k_ref[0, pl.ds(off, tk), pair * LANES:(pair + 1) * LANES],
                                ka_ref[0, pl.ds(off, tk), :]], axis=1)

    def v_tile(hh, kv):
        return v_ref[0, kv, hh * V_ROWS:(hh + 1) * V_ROWS, :]

    _flash_heads(hg, lhs_tile, rhs_ref, v_tile, None, qi, causal, ot_ref, m_ref, acc_ref, lg_refs, start)
    o_ref[0] = ot_ref[...].T.astype(o_ref.dtype)


def _fox_prompt(qt, kr, kaug, vg, ct, qn, kn, tq, tk, hg):
    b, _, t = qt.shape
    nk = t // tk
    c_first = ct[:, :, 0::tk]
    c_last = ct[:, :, tk - 1::tk]
    table = pl.BlockSpec((1, N_HEADS, nk), lambda i, g, j: (i, 0, 0))
    return pl.pallas_call(
        functools.partial(_fox_body, tq=tq, tk=tk, hg=hg),
        grid=(b, N_HEADS // hg, t // tq),
        in_specs=[
            table, table, table, table,
            pl.BlockSpec((1, hg * D_HEAD, tq), lambda i, g, j: (i, g, j)),
            pl.BlockSpec((1, t, hg * D_HEAD), lambda i, g, j: (i, 0, g)),
            pl.BlockSpec((1, t, LANES), lambda i, g, j: (i, 0, 0)),
            pl.BlockSpec((1, nk, hg * V_ROWS, tk), lambda i, g, j: (i, 0, g, 0)),
            pl.BlockSpec((1, N_HEADS, tq), lambda i, g, j: (i, 0, j)),
        ],
        out_specs=pl.BlockSpec((1, tq, hg * D_HEAD), lambda i, g, j: (i, j, g)),
        out_shape=jax.ShapeDtypeStruct((b, t, BRANCH_W), BF16),
        scratch_shapes=_attention_scratch(hg, tq, tk),
        compiler_params=_cparams("arbitrary", "arbitrary", "arbitrary"),
        name="fox_prompt",
    )(qn, kn, c_first, c_last, qt, kr, kaug, vg, ct)


def _first_live_tile_alibi(sl_ref, qn_ref, kn_ref, h0, n_heads, qi, tk):
    kn = kn_ref[0]
    tile = lax.broadcasted_iota(jnp.int32, kn.shape, 1)
    head = lax.broadcasted_iota(jnp.int32, kn.shape, 0)
    at_qi = lambda a: jnp.sum(jnp.where(tile == qi, a, 0.0), axis=1, keepdims=True)
    slope = jnp.zeros(kn.shape, F32)
    for hh in range(n_heads):
        slope = jnp.where(head == h0 + hh, sl_ref[h0 + hh], slope)
    gap = ((qi - tile - 1) * tk + 1).astype(F32)
    ub = (at_qi(qn_ref[0]) * 1.02) * (kn + at_qi(kn)) - slope * LOG2E * gap + 0.1
    live = (ub > -SKIP_MARGIN) & (tile < qi) & (head >= h0) & (head < h0 + n_heads)
    return jnp.min(jnp.where(live, tile, qi))


def _moba_body(sl_ref, qn_ref, kn_ref, q_ref, k_ref, v_ref, km_ref, o_ref, sel_ref, pa_ref, rhs_ref, ot_ref, m_ref, acc_ref,
               *lg_refs, tq, tk, nb, hg):
    qi = pl.program_id(2)
    n_late = min(MOBA_LATE_HEADS, hg)
    start_late = _first_live_tile_alibi(sl_ref, qn_ref, kn_ref, pl.program_id(1) * hg, n_late, qi, tk)

    @pl.when((pl.program_id(0) == 0) & (pl.program_id(1) == 0) & (qi == 0))
    def _():
        shape = pa_ref.shape
        pos = lax.broadcasted_iota(jnp.int32, shape, 0).astype(F32)
        col = lax.broadcasted_iota(jnp.int32, shape, 1)
        p_hi = pos.astype(BF16).astype(F32)
        pa_ref[...] = jnp.where(col < 3, p_hi, jnp.where(col < 6, pos - p_hi, jnp.where(col < 9, 1.0, 0.0))).astype(BF16)

    rowi = lax.broadcasted_iota(jnp.int32, (LANES, tq), 0)
    causal = (lax.broadcasted_iota(jnp.int32, (tk, tq), 0) <= lax.broadcasted_iota(jnp.int32, (tk, tq), 1))
    blk = lax.broadcasted_iota(jnp.int32, (nb, tq), 0)
    tpos = (qi * tq + lax.broadcasted_iota(jnp.int32, (1, tq), 1)).astype(F32)
    for hh in range(hg):
        h = pl.program_id(1) * hg + hh
        qm = _masked_q(q_ref, hh, tq)

        pair = hh // 2
        km_hi, km_lo = _split2(km_ref[0][:, pair * LANES:(pair + 1) * LANES])
        sc = _dot(km_hi, qm) + _dot(km_lo, qm)
        rank = jnp.zeros((nb, tq), F32)
        for mm in range(nb):
            row = sc[mm:mm + 1, :]
            live = (mm < qi).astype(F32)
            rank = rank + jnp.where(blk > mm, jnp.where(row >= sc, live, 0.0), jnp.where(row > sc, live, 0.0))
        chosen = (blk < qi) & (rank < MOBA_TOPK)
        sel_ref[hh] = jnp.where(chosen, 0.0, NEG_INF)

        slope = jnp.full((1, tq), sl_ref[h], F32) * LOG2E
        s_hi, s_mid, s_lo = _split3(slope)
        a_hi, a_mid, a_lo = _split3(-(slope * tpos))
        aug = jnp.where((rowi == 0) | (rowi == 3), s_hi, 0.0)
        aug = jnp.where((rowi == 1) | (rowi == 4), s_mid, aug)
        aug = jnp.where((rowi == 2) | (rowi == 5), s_lo, aug)
        aug = jnp.where(rowi == 6, a_hi, aug)
        aug = jnp.where(rowi == 7, a_mid, aug)
        aug = jnp.where(rowi == 8, a_lo, aug)
        rhs_ref[hh, 0:LANES, :] = qm
        rhs_ref[hh, LANES:2 * LANES, :] = aug.astype(BF16)

    def lhs_tile(pair, kv):
        off = pl.multiple_of(kv * tk, tk)
        return jnp.concatenate([k_ref[0, pl.ds(off, tk), pair * LANES:(pair + 1) * LANES],
                                pa_ref[pl.ds(off, tk), :]], axis=1)

    def v_tile(hh, kv):
        return v_ref[0, kv, hh * V_ROWS:(hh + 1) * V_ROWS, :]

    def past_bias(hh, kv):
        return sel_ref[hh, pl.ds(kv, 1), :]

    _flash_heads(hg, lhs_tile, rhs_ref, v_tile, past_bias, qi, causal, ot_ref, m_ref, acc_ref, lg_refs,
                 start=0, n_late=n_late, start_late=start_late)
    o_ref[0] = ot_ref[...].T.astype(o_ref.dtype)


def _moba_prompt(slopes, qt, kr, vg, kmean, qn, kn, tq, tk, hg):
    b, _, t = qt.shape
    nk = t // tk
    nb = kmean.shape[1]
    table = pl.BlockSpec((1, N_HEADS, nk), lambda i, g, j: (i, 0, 0))
    return pl.pallas_call(
        functools.partial(_moba_body, tq=tq, tk=tk, nb=nb, hg=hg),
        grid=(b, N_HEADS // hg, t // tq),
        in_specs=[
            pl.BlockSpec(memory_space=pltpu.SMEM), table, table,
            pl.BlockSpec((1, hg * D_HEAD, tq), lambda i, g, j: (i, g, j)),
            pl.BlockSpec((1, t, hg * D_HEAD), lambda i, g, j: (i, 0, g)),
            pl.BlockSpec((1, nk, hg * V_ROWS, tk), lambda i, g, j: (i, 0, g, 0)),
            pl.BlockSpec((1, nb, hg * D_HEAD), lambda i, g, j: (i, 0, g)),
        ],
        out_specs=pl.BlockSpec((1, tq, hg * D_HEAD), lambda i, g, j: (i, j, g)),
        scratch_shapes=[pltpu.VMEM((hg, nb, tq), F32), pltpu.VMEM((t, LANES), BF16)] + _attention_scratch(hg, tq, tk),
        out_shape=jax.ShapeDtypeStruct((b, t, BRANCH_W), BF16),
        compiler_params=_cparams("arbitrary", "arbitrary", "arbitrary"),
        name="moba_prompt",
    )(slopes, qn, kn, qt, kr, vg, kmean)


def _combine_body(x_ref, sc_ref, sh_ref, g_ref, oa_ref, ob_ref, wz_ref, wg_ref, wpa_ref, wpb_ref, wo_ref,
                  lng_ref, lnb_ref, y_ref, *, alpha):
    x = x_ref[0]
    hb = (x * (1.0 + sc_ref[0]) + sh_ref[0]).astype(BF16)
    z = _dot(hb, wz_ref[...])
    g = _dot(hb, wg_ref[...])
    za, zb = z[:, :BRANCH_W], z[:, BRANCH_W:]
    d = x.shape[-1]
    ua = (oa_ref[0].astype(F32) * (za * _sigmoid(za))).astype(BF16)
    ub = (ob_ref[0].astype(F32) * (zb * _sigmoid(zb))).astype(BF16)
    ya = _dot(ua, wpa_ref[...])
    yb = _dot(ub, wpb_ref[...])
    mix = (_sigmoid(g[:, :d]) * ya + _sigmoid(g[:, d:]) * yb).astype(BF16)
    s = _dot(mix, wo_ref[...])
    r = alpha * x + g_ref[0] * s
    mu = jnp.mean(r, axis=-1, keepdims=True)
    dev = r - mu
    var = jnp.mean(dev * dev, axis=-1, keepdims=True)
    y_ref[0] = dev * lax.rsqrt(var + LN_EPS) * lng_ref[...] + lnb_ref[...]


def _combine(x, scale, shift, gate, oa, ob, wp, alpha, tm):
    b, t, d = x.shape
    tmod = scale.shape[1]
    rmap = lambda i, j: (i, j, 0)
    mod_spec = pl.BlockSpec((1, 1, d), lambda i, j: (i, 0, 0)) if tmod == 1 else pl.BlockSpec((1, tm, d), rmap)
    const = lambda i, j: (0, 0)
    names = ("w_z", "w_g", "w_pa", "w_pb", "w_o", "ln_g", "ln_b")
    return pl.pallas_call(
        functools.partial(_combine_body, alpha=alpha),
        grid=(b, t // tm),
        in_specs=[pl.BlockSpec((1, tm, d), rmap), mod_spec, mod_spec, mod_spec,
                  pl.BlockSpec((1, tm, BRANCH_W), rmap), pl.BlockSpec((1, tm, BRANCH_W), rmap)]
                 + [pl.BlockSpec(wp[k].shape, const) for k in names],
        out_specs=pl.BlockSpec((1, tm, d), rmap),
        out_shape=jax.ShapeDtypeStruct((b, t, d), F32),
        compiler_params=_cparams("arbitrary", "arbitrary"),
        name="combine",
    )(x, scale, shift, gate, oa, ob, *[wp[k] for k in names])


def _proj_sample_body(x_ref, sc_ref, sh_ref, wr_ref, wt_ref, wft_ref, wfr_ref, bf_ref, bfr_ref,
                      qa_ref, ka_ref, va_ref, qb_ref, kb_ref, vb_ref, lfr_ref,
                      kat_ref, vat_ref, kbt_ref, vbt_ref, lft_ref):
    h32 = x_ref[...] * (1.0 + sc_ref[...]) + sh_ref[...]
    hb = h32.astype(BF16)
    hl = (h32 - hb.astype(F32)).astype(BF16)
    r = _dot(hb, wr_ref[...])
    scale = D_HEAD ** -0.5
    w = BRANCH_W
    qa_ref[...] = r[:, 0:w] * scale
    ka_ref[...] = r[:, w:2 * w]
    va_ref[...] = r[:, 2 * w:3 * w]
    qb_ref[...] = r[:, 3 * w:4 * w] * scale
    kb_ref[...] = r[:, 4 * w:5 * w]
    vb_ref[...] = r[:, 5 * w:6 * w]
    wr_hi, wr_lo = _split2(wfr_ref[...])
    flr = _dot(hb, wr_hi) + _dot(hl, wr_hi) + _dot(hb, wr_lo)
    lfr_ref[...] = _log_sigmoid(flr + bfr_ref[...])
    for g, ref in ((1, kat_ref), (2, vat_ref), (4, kbt_ref), (5, vbt_ref)):
        ref[...] = _dot_nt(wt_ref[g * w:(g + 1) * w, :], hb)
    lft_ref[...] = _log_sigmoid(_forget_logits_t(wft_ref, hb, hl) + bf_ref[:, 0:1])


def _project_sample(x, scale, shift, wp):
    n, d = x.shape
    row = jax.ShapeDtypeStruct((n, BRANCH_W), F32)
    col = jax.ShapeDtypeStruct((BRANCH_W, n), F32)
    out_shape = (row,) * 6 + (jax.ShapeDtypeStruct((n, LANES), F32),) + (col,) * 4 + (
        jax.ShapeDtypeStruct((N_HEADS, n), F32),)
    args = (x, scale, shift, wp["w_rows"], wp["w_t"], wp["wft"], wp["wfr"], wp["bf_col"], wp["bf_row"])
    return pl.pallas_call(
        _proj_sample_body,
        out_shape=out_shape,
        compiler_params=pltpu.CompilerParams(vmem_limit_bytes=VMEM_LIMIT),
        name="proj_sample",
    )(*args)


def _shift_right(n):
    return n.bit_length() - 1


def _sample_body(pt_ref, qa_ref, qb_ref, *refs, past_len, nstep, nbs, ppb, dec_seq, page):
    pps = nbs * ppb
    caches = refs[:5 * pps]
    mk, mv, fk, fv, fl = (caches[i * pps:(i + 1) * pps] for i in range(5))
    kan_ref, van_ref, kbn_ref, vbn_ref, lfn_ref, slope_ref, oa_ref, ob_ref = refs[5 * pps:5 * pps + 8]
    qda_ref, qdb_ref, ssc_ref, sm_ref, sl_ref, so_ref, fm_ref, fls_ref, facc_ref, fcar_ref = refs[5 * pps + 8:]
    del pt_ref
    b = pl.program_id(0)
    n = pl.program_id(1)
    nrow = dec_seq * N_HEADS
    nkey = pps * page
    ntok = kan_ref.shape[1]
    qsh = _shift_right(N_HEADS)

    def rows_iota(shape):
        return lax.broadcasted_iota(jnp.int32, shape, 0)

    def lanes_iota(shape):
        return lax.broadcasted_iota(jnp.int32, shape, 1)

    head_mask = (lanes_iota((nrow, BRANCH_W)) >> _shift_right(D_HEAD)) == (rows_iota((nrow, BRANCH_W)) & (N_HEADS - 1))

    def block_diag_q(q_ref):
        q = q_ref[0]
        q32 = jnp.concatenate([jnp.broadcast_to(q[i:i + 1, :], (N_HEADS, BRANCH_W)) for i in range(dec_seq)], axis=0)
        q32 = jnp.where(head_mask, q32, 0.0)
        hi = q32.astype(BF16)
        lo = (q32 - hi.astype(F32)).astype(BF16)
        return jnp.concatenate([hi, lo], axis=0)

    @pl.when(n == 0)
    def _():
        qda_ref[...] = block_diag_q(qa_ref)
        qdb_ref[...] = block_diag_q(qb_ref)
        ssc_ref[...] = jnp.full(ssc_ref.shape, NEG_INF, F32)
        sm_ref[...] = jnp.full(sm_ref.shape, NEG_INF, F32)
        sl_ref[...] = jnp.zeros(sl_ref.shape, F32)
        so_ref[...] = jnp.zeros(so_ref.shape, F32)
        fm_ref[...] = jnp.full(fm_ref.shape, NEG_INF, F32)
        fls_ref[...] = jnp.zeros(fls_ref.shape, F32)
        facc_ref[...] = jnp.zeros(facc_ref.shape, F32)
        fcar_ref[...] = jnp.zeros(fcar_ref.shape, F32)

    def qk(qd_ref, kt):
        s2 = _dot(qd_ref[...], kt)
        return s2[:nrow] + s2[nrow:]

    def bcast(col):
        return jnp.broadcast_to(col, (nrow, LANES))

    def softmax_part(lg):
        m = jnp.max(lg, axis=1, keepdims=True)
        p = jnp.exp(lg - m).astype(BF16)
        return m, p, jnp.sum(p.astype(F32), axis=1, keepdims=True)

    slope = slope_ref[:, 0:1]
    qrow = rows_iota((nrow, MOBA_BLOCK)) >> qsh

    def block_t(refs, j):
        return jnp.concatenate([r[0] for r in refs[j * ppb:(j + 1) * ppb]], axis=1).astype(BF16)

    s_moba = [qk(qda_ref, block_t(mk, j)) for j in range(nbs)]
    s_fox = [qk(qdb_ref, block_t(fk, j)) for j in range(nbs)]

    parts = []
    for j in range(nbs):
        score = jnp.sum(s_moba[j], axis=1, keepdims=True)
        dist = (past_len - (n * nbs + j) * MOBA_BLOCK + qrow - lanes_iota((nrow, MOBA_BLOCK))).astype(F32)
        m_n, p, l_n = softmax_part(s_moba[j] - slope * dist)
        parts.append((score, m_n, l_n, p))

    lf = jnp.concatenate([r[0] for r in fl], axis=1)
    c_blk = _lane_cumsum(lf) + fcar_ref[:, 0:1]
    fcar_ref[...] = jnp.broadcast_to(c_blk[:, nkey - 1:nkey], fcar_ref.shape)
    c_rows = jnp.concatenate([c_blk] * dec_seq, axis=0)
    u = jnp.concatenate(s_fox, axis=1) - c_rows
    m_old = fm_ref[:, 0:1]
    m_new = jnp.maximum(m_old, jnp.max(u, axis=1, keepdims=True))
    alpha = jnp.exp(m_old - m_new)
    p_fox = jnp.exp(u - m_new).astype(BF16)
    fls_ref[...] = bcast(alpha * fls_ref[:, 0:1] + jnp.sum(p_fox.astype(F32), axis=1, keepdims=True))
    fm_ref[...] = bcast(m_new)

    o_moba = [_dot_nt(parts[j][3], block_t(mv, j)) for j in range(nbs)]
    o_fox = _dot_nt(p_fox[:, 0:MOBA_BLOCK], block_t(fv, 0))
    for j in range(1, nbs):
        o_fox = o_fox + _dot_nt(p_fox[:, j * MOBA_BLOCK:(j + 1) * MOBA_BLOCK], block_t(fv, j))
    facc_ref[...] = alpha * facc_ref[...] + o_fox

    state = (ssc_ref, sm_ref, sl_ref, so_ref)
    slots = [[ref[i] for ref in state] for i in range(MOBA_TOPK)]
    for j in range(nbs):
        score, m_n, l_n, _ = parts[j]
        new = (bcast(score), bcast(m_n), bcast(l_n), o_moba[j])
        gt = [score > slots[i][0][:, 0:1] for i in range(MOBA_TOPK)]
        nxt = []
        for i in range(MOBA_TOPK):
            row = []
            for f in range(len(state)):
                keep = jnp.where(gt[i], new[f], slots[i][f])
                row.append(keep if i == 0 else jnp.where(gt[i - 1], slots[i - 1][f], keep))
            nxt.append(row)
        slots = nxt
    for i in range(MOBA_TOPK):
        for f, ref in enumerate(state):
            ref[i] = slots[i][f]

    @pl.when(n == nstep - 1)
    def _():
        lane = lanes_iota((nrow, ntok))
        qr = rows_iota((nrow, ntok)) >> qsh
        tok_q = lane & (dec_seq - 1)
        visible = ((lane >> _shift_right(dec_seq)) == b) & (tok_q <= qr)

        def collapse(o):
            return jnp.sum(jnp.where(head_mask, o, 0.0).reshape(dec_seq, N_HEADS, BRANCH_W), axis=1)

        s_own = qk(qda_ref, kan_ref[...].astype(BF16))
        lg = jnp.where(visible, s_own - slope * (qr - tok_q).astype(F32), NEG_INF)
        m_o, p_o, l_o = softmax_part(lg)
        o_o = _dot_nt(p_o, van_ref[...].astype(BF16))
        ms = [sm_ref[i][:, 0:1] for i in range(MOBA_TOPK)]
        m_tot = m_o
        for mi in ms:
            m_tot = jnp.maximum(m_tot, mi)
        w_o = jnp.exp(m_o - m_tot)
        num = w_o * o_o
        den = w_o * l_o
        for i in range(MOBA_TOPK):
            w_i = jnp.exp(ms[i] - m_tot)
            num = num + w_i * so_ref[i]
            den = den + w_i * sl_ref[i][:, 0:1]
        oa_ref[0] = collapse(num / den)

        c_new = _lane_cumsum(lfn_ref[...], seg=dec_seq)
        c_new_rows = jnp.concatenate([c_new] * dec_seq, axis=0)
        cq = jnp.sum(jnp.where(lane == b * dec_seq + qr, c_new_rows, 0.0), axis=1, keepdims=True)
        c_past = jnp.concatenate([fcar_ref[:, 0:1]] * dec_seq, axis=0)
        m_p = fm_ref[:, 0:1] + c_past + cq
        s_new = qk(qdb_ref, kbn_ref[...].astype(BF16))
        lg = jnp.where(visible, s_new + cq - c_new_rows, NEG_INF)
        m_o, p_o, l_o = softmax_part(lg)
        o_o = _dot_nt(p_o, vbn_ref[...].astype(BF16))
        m_tot = jnp.maximum(m_p, m_o)
        w_p = jnp.exp(m_p - m_tot)
        w_o = jnp.exp(m_o - m_tot)
        num = w_p * facc_ref[...] + w_o * o_o
        den = w_p * fls_ref[:, 0:1] + w_o * l_o
        ob_ref[0] = collapse(num / den)


def _sample_attention(page_table, qa, qb, caches, new_t, slope_rows, past_len, page):
    nb_, dec_seq, _ = qa.shape
    ppb = MOBA_BLOCK // page
    nblk = past_len // MOBA_BLOCK
    nbs = next(c for c in (SAMPLE_BLOCKS_PER_STEP, 2, 1) if nblk % c == 0)
    pps = nbs * ppb
    nrow = dec_seq * N_HEADS

    def page_specs(rows):
        return [pl.BlockSpec((1, rows, page), functools.partial(lambda i, n, pt, j: (pt[i, n * pps + j], 0, 0), j=j))
                for j in range(pps)]

    q_spec = pl.BlockSpec((1, dec_seq, BRANCH_W), lambda i, n, pt: (i, 0, 0))
    full = lambda a: pl.BlockSpec(a.shape, lambda i, n, pt: (0, 0))
    in_specs = [q_spec, q_spec]
    args = [qa, qb]
    for c in caches:
        in_specs += page_specs(c.shape[1])
        args += [c] * pps
    in_specs += [full(a) for a in new_t] + [full(slope_rows)]
    args += list(new_t) + [slope_rows]
    vm = lambda *s: pltpu.VMEM(s, F32)
    return pl.pallas_call(
        functools.partial(_sample_body, past_len=past_len, nstep=nblk // nbs, nbs=nbs, ppb=ppb, dec_seq=dec_seq,
                          page=page),
        grid_spec=pltpu.PrefetchScalarGridSpec(
            num_scalar_prefetch=1,
            grid=(nb_, nblk // nbs),
            in_specs=in_specs,
            out_specs=(q_spec, q_spec),
            scratch_shapes=[
                pltpu.VMEM((2 * nrow, BRANCH_W), BF16), pltpu.VMEM((2 * nrow, BRANCH_W), BF16),
                vm(MOBA_TOPK, nrow, LANES), vm(MOBA_TOPK, nrow, LANES), vm(MOBA_TOPK, nrow, LANES),
                vm(MOBA_TOPK, nrow, BRANCH_W),
                vm(nrow, LANES), vm(nrow, LANES), vm(nrow, BRANCH_W), vm(N_HEADS, LANES)],
        ),
        out_shape=(jax.ShapeDtypeStruct(qa.shape, F32), jax.ShapeDtypeStruct(qa.shape, F32)),
        compiler_params=_cparams("arbitrary", "arbitrary"),
        name="sample_attn",
    )(page_table, *args)


def _prep_weights(w_in, b_f, w_pa, w_pb, w_o, ln_g, ln_b):
    d = w_in.shape[0]
    w = BRANCH_W
    cols = {}
    off = 0
    for name, size in (("qa", w), ("ka", w), ("va", w), ("za", w), ("qb", w), ("kb", w), ("vb", w), ("zb", w),
                       ("f", N_HEADS), ("ga", d), ("gb", d)):
        cols[name] = w_in[:, off:off + size]
        off += size
    w_rows = jnp.concatenate([cols[k] for k in ("qa", "ka", "va", "qb", "kb", "vb")], axis=1).astype(BF16)
    wf = cols["f"]
    return {
        "w_rows": w_rows,
        "w_t": w_rows.T,
        "wft": jnp.pad(wf.T, ((0, N_HEADS), (0, 0))),
        "wfr": jnp.pad(wf, ((0, 0), (0, LANES - N_HEADS))),
        "bf_col": jnp.broadcast_to(b_f[:, None], (N_HEADS, LANES)),
        "bf_row": jnp.pad(b_f[None, :], ((0, 0), (0, LANES - N_HEADS))),
        "w_z": jnp.concatenate([cols["za"], cols["zb"]], axis=1).astype(BF16),
        "w_g": jnp.concatenate([cols["ga"], cols["gb"]], axis=1).astype(BF16),
        "w_pa": w_pa.astype(BF16), "w_pb": w_pb.astype(BF16), "w_o": w_o.astype(BF16),
        "ln_g": ln_g[None, :], "ln_b": ln_b[None, :],
    }


def _alibi_slopes():
    return 2.0 ** (-8.0 * (jnp.arange(N_HEADS, dtype=F32) + 1.0) / N_HEADS)


def _heads_last(a_t, b, t):
    return a_t.reshape(b, N_HEADS, D_HEAD, t).transpose(0, 3, 1, 2)


def kernel(x_prompt, x_sample, cache_moba_k, cache_moba_v, cache_fox_k, cache_fox_v, cache_fox_logf, page_table,
           c_prompt, c_sample, w_ada, b_ada, w_in, b_f, w_pa, w_pb, w_o, ln_g, ln_b):
    depth = w_ada.shape[0]
    bp, t, d = x_prompt.shape
    bs, dec_seq, _ = x_sample.shape
    n_pool, page = cache_moba_k.shape[1], cache_moba_k.shape[2]
    past_len = page_table.shape[1] * page
    assert t % MOBA_BLOCK == 0 and past_len % MOBA_BLOCK == 0 and MOBA_BLOCK % page == 0
    assert dec_seq & (dec_seq - 1) == 0 and (bs * dec_seq) % 8 == 0
    alpha = (2.0 * depth) ** 0.25
    tq = tk = MOBA_BLOCK
    tm = 512 if t % 512 == 0 else MOBA_BLOCK

    slopes = _alibi_slopes()
    slope_rows = jnp.broadcast_to(jnp.tile(slopes, dec_seq)[:, None], (dec_seq * N_HEADS, LANES))

    x_p = x_prompt
    x_s = x_sample.reshape(1, bs * dec_seq, d)
    outs = [[] for _ in range(10)]
    for l in range(depth):
        wp = _prep_weights(w_in[l], b_f[l], w_pa[l], w_pb[l], w_o[l], ln_g[l], ln_b[l])
        mod = _ada(jnp.concatenate([c_prompt, c_sample], axis=0), w_ada[l], b_ada[l])
        shift, scale, gate = (mod[:, i * d:(i + 1) * d] for i in range(3))
        mod_p = [m[:bp, None, :] for m in (scale, shift, gate)]
        mod_s = [jnp.repeat(m[bp:], dec_seq, axis=0)[None] for m in (scale, shift, gate)]

        (kat, vat, kbt, vbt, qat, qbt, vag, vbg, kar, kbr, lft, ct, fkaug, kmean, qn, kn, qna, kna) = _project_prompt(
            x_p, mod_p[0], mod_p[1], wp, tm, tk)
        nblk_tile = tm // MOBA_BLOCK
        kmean = kmean[:, :, :nblk_tile, :].reshape(bp, t // MOBA_BLOCK, BRANCH_W)
        per_tile = lambda a: a[:, :, :, :tm // tk].transpose(0, 2, 1, 3).reshape(bp, N_HEADS, t // tk)
        oa = _moba_prompt(slopes, qat, kar, vag, kmean, per_tile(qna), per_tile(kna), tq, tk, HEADS_PER_STEP)
        ob = _fox_prompt(qbt, kbr, fkaug, vbg, ct, per_tile(qn), per_tile(kn), tq, tk, HEADS_PER_STEP)
        x_p = _combine(x_p, mod_p[0], mod_p[1], mod_p[2], oa, ob, wp, alpha, tm)
        for i, a in enumerate((kat, vat, kbt, vbt)):
            outs[i].append(_heads_last(a, bp, t))
        outs[4].append(lft.transpose(0, 2, 1))

        n_s = bs * dec_seq
        (qa_s, ka_s, va_s, qb_s, kb_s, vb_s, lfr_s, kat_s, vat_s, kbt_s, vbt_s, lft_s) = _project_sample(
            x_s[0], mod_s[0][0], mod_s[1][0], wp)
        feat_major = lambda c: c[l].transpose(0, 2, 3, 1).reshape(n_pool, BRANCH_W, page)
        caches = [feat_major(c) for c in (cache_moba_k, cache_moba_v, cache_fox_k, cache_fox_v)]
        caches.append(cache_fox_logf[l].transpose(0, 2, 1))
        oa_s, ob_s = _sample_attention(page_table, qa_s.reshape(bs, dec_seq, BRANCH_W),
                                       qb_s.reshape(bs, dec_seq, BRANCH_W), caches,
                                       (kat_s, vat_s, kbt_s, vbt_s, lft_s), slope_rows, past_len, page)
        x_s = _combine(x_s, mod_s[0], mod_s[1], mod_s[2], oa_s.reshape(1, n_s, BRANCH_W),
                       ob_s.reshape(1, n_s, BRANCH_W), wp, alpha, n_s)
        for i, a in enumerate((ka_s, va_s, kb_s, vb_s)):
            outs[5 + i].append(a.reshape(bs, dec_seq, N_HEADS, D_HEAD))
        outs[9].append(lfr_s[:, :N_HEADS].reshape(bs, dec_seq, N_HEADS))

    return (x_p, x_s.reshape(bs, dec_seq, d)) + tuple(jnp.stack(o) for o in outs)
```
